```python
import jax, jax.numpy as jnp
from jax import lax
import numpy as np

D_MODEL = 1024
BATCH = 2
SEQ = 8192
DEPTH = 4
DEC_BATCH = 128
DEC_SEQ = 8
PAST_LEN = 8192
PAGE_SIZE = 128

HEAD_DIM = 64
MIX_WIDTH = D_MODEL
A_WIDTH = MIX_WIDTH // 2
B_WIDTH = MIX_WIDTH - A_WIDTH
A_HEADS = A_WIDTH // HEAD_DIM
N_HEADS = B_WIDTH // HEAD_DIM
KV_HEADS = 2
GQA_GROUP = N_HEADS // KV_HEADS
CHUNK = 128
WINDOW = 128
BLOCK = WINDOW
ROPE_THETA = 500000.0
ROT_DIM = HEAD_DIM // 4
D_FF = 4 * D_MODEL
EPS = 1e-6
Q_COLS = N_HEADS * HEAD_DIM
KV_COLS = KV_HEADS * HEAD_DIM
IN_COLS = 2 * A_WIDTH + Q_COLS + 2 * KV_COLS
SPLITS = (A_WIDTH, 2 * A_WIDTH, 2 * A_WIDTH + Q_COLS, 2 * A_WIDTH + Q_COLS + KV_COLS)

kernel_name = "hymba_chunkgmlp_swa_sink_decoder_step"


def rms_norm(x, g):
    xf = x.astype(jnp.float32)
    y = xf * lax.rsqrt(jnp.mean(xf * xf, axis=-1, keepdims=True) + EPS) * g.astype(jnp.float32)
    return y.astype(x.dtype)


def rope(x, pos):
    half = ROT_DIM // 2
    inv = jnp.power(jnp.float32(ROPE_THETA), -2.0 * jnp.arange(half, dtype=jnp.float32) / ROT_DIM)
    ang = pos.astype(jnp.float32)[:, None] * inv[None, :]
    cos = jnp.cos(ang)[:, None, :]
    sin = jnp.sin(ang)[:, None, :]
    xf = x[..., :ROT_DIM].astype(jnp.float32)
    x1, x2 = xf[..., :half], xf[..., half:]
    rot = jnp.concatenate([x1 * cos - x2 * sin, x2 * cos + x1 * sin], axis=-1).astype(x.dtype)
    return jnp.concatenate([rot, x[..., ROT_DIM:]], axis=-1)


def mixer_inputs(x, pos, g_mix, w_in, g_sv, g_q, g_k):
    lead = x.shape[:-1]
    z = rms_norm(x, g_mix) @ w_in
    z_u, z_v, z_q, z_k, z_kv = jnp.split(z, SPLITS, axis=-1)
    u = jax.nn.gelu(z_u, approximate=False)
    va = rms_norm(jax.nn.gelu(z_v, approximate=False), g_sv).reshape(*lead, A_HEADS, HEAD_DIM)
    q = rope(rms_norm(z_q.reshape(*lead, N_HEADS, HEAD_DIM), g_q), pos)
    k = rope(rms_norm(z_k.reshape(*lead, KV_HEADS, HEAD_DIM), g_k), pos)
    v = z_kv.reshape(*lead, KV_HEADS, HEAD_DIM)
    return u, va, q, k, v


def attn_core(q, k_ctx, v_ctx, mask, sinks):
    qg = q.reshape(*q.shape[:-2], KV_HEADS, GQA_GROUP, HEAD_DIM)
    s = jnp.einsum('...qkgd,...jkd->...kgqj', qg, k_ctx,
                   preferred_element_type=jnp.float32) * (HEAD_DIM ** -0.5)
    s = jnp.where(mask, s, jnp.float32(-1e30))
    sink = jnp.broadcast_to(sinks.astype(jnp.float32).reshape(KV_HEADS, GQA_GROUP, 1, 1),
                            s.shape[:-1] + (1,))
    p = jax.nn.softmax(jnp.concatenate([s, sink], axis=-1), axis=-1)[..., :-1]
    o = jnp.einsum('...kgqj,...jkd->...qkgd', p.astype(v_ctx.dtype), v_ctx)
    return o.reshape(*o.shape[:-3], N_HEADS * HEAD_DIM)


def with_prev_block(t):
    prev = jnp.concatenate([jnp.zeros_like(t[:, :1]), t[:, :-1]], axis=1)
    return jnp.concatenate([prev, t], axis=2)


def merge_and_ffn(x, ya, yb, g_oa, g_ob, w_out, g_ffn, w_up, w_down):
    o = jnp.concatenate([rms_norm(ya, g_oa), rms_norm(yb, g_ob)], axis=-1) @ w_out
    x = x + o
    h = rms_norm(x, g_ffn) @ w_up
    return x + jnp.square(jax.nn.relu(h)) @ w_down


def setup_inputs(seed: int = 0) -> dict:
    key = jax.random.key(seed)
    ks = jax.random.split(key, 20)
    f32 = jnp.float32
    nrm = lambda k, s: jax.random.normal(k, s, f32)
    return {
        "x_prompt": nrm(ks[0], (BATCH, SEQ, D_MODEL)),
        "x_sample": nrm(ks[1], (DEC_BATCH, DEC_SEQ, D_MODEL)),
        "cache_win_k": nrm(ks[2], (DEPTH, DEC_BATCH, WINDOW, KV_HEADS, HEAD_DIM)),
        "cache_win_v": nrm(ks[3], (DEPTH, DEC_BATCH, WINDOW, KV_HEADS, HEAD_DIM)),
        "g_mix": 1.0 + 0.05 * nrm(ks[4], (DEPTH, D_MODEL)),
        "w_in": nrm(ks[5], (DEPTH, D_MODEL, IN_COLS)) * D_MODEL ** -0.5,
        "g_sv": 1.0 + 0.05 * nrm(ks[6], (DEPTH, A_WIDTH)),
        "g_q": 1.0 + 0.05 * nrm(ks[7], (DEPTH, HEAD_DIM)),
        "g_k": 1.0 + 0.05 * nrm(ks[8], (DEPTH, HEAD_DIM)),
        "w_spatial": nrm(ks[9], (DEPTH, A_HEADS, CHUNK, CHUNK)) * CHUNK ** -0.5,
        "b_spatial": 1.0 + 0.1 * nrm(ks[10], (DEPTH, A_HEADS, CHUNK)),
        "sinks": 0.5 * nrm(ks[11], (DEPTH, N_HEADS)),
        "g_out_a": 1.0 + 0.05 * nrm(ks[12], (DEPTH, A_WIDTH)),
        "g_out_b": 1.0 + 0.05 * nrm(ks[13], (DEPTH, B_WIDTH)),
        "w_out": nrm(ks[14], (DEPTH, MIX_WIDTH, D_MODEL)) * MIX_WIDTH ** -0.5,
        "g_ffn": 1.0 + 0.05 * nrm(ks[15], (DEPTH, D_MODEL)),
        "w_up": nrm(ks[16], (DEPTH, D_MODEL, D_FF)) * D_MODEL ** -0.5,
        "w_down": nrm(ks[17], (DEPTH, D_FF, D_MODEL)) * (0.5 * D_FF ** -0.5),
    }


def reference(x_prompt, x_sample, cache_win_k, cache_win_v, g_mix, w_in, g_sv, g_q, g_k,
              w_spatial, b_spatial, sinks, g_out_a, g_out_b, w_out, g_ffn, w_up, w_down):
    n_blocks = SEQ // BLOCK
    n_chunks = SEQ // CHUNK
    pos_p = jnp.arange(SEQ, dtype=jnp.int32)
    pos_s = PAST_LEN + jnp.arange(DEC_SEQ, dtype=jnp.int32)
    tril = jnp.tril(jnp.ones((CHUNK, CHUNK), dtype=bool))

    qi = jnp.arange(BLOCK)[:, None]
    kj = jnp.arange(2 * BLOCK)[None, :]
    dist = qi + BLOCK - kj
    band = (dist >= 0) & (dist <= WINDOW)
    key_pos = (jnp.arange(n_blocks)[:, None] - 1) * BLOCK + jnp.arange(2 * BLOCK)[None, :]
    mask_p = (band[None] & (key_pos >= 0)[:, None, :])[:, None, None]
    si = jnp.arange(DEC_SEQ)[:, None]
    sj = jnp.arange(WINDOW + DEC_SEQ)[None, :]
    sd = si + WINDOW - sj
    mask_s = (sd >= 0) & (sd <= WINDOW)

    xp, xs = x_prompt, x_sample
    nk_p, nv_p, nk_s, nv_s, nchunk_v = [], [], [], [], []
    for l in range(DEPTH):
        ws = jnp.where(tril, w_spatial[l], jnp.zeros_like(w_spatial[l]))
        bias_tc = b_spatial[l].T[:, :, None]

        u, va, q, k, v = mixer_inputs(xp, pos_p, g_mix[l], w_in[l], g_sv[l], g_q[l], g_k[l])
        vch = va.reshape(BATCH, n_chunks, CHUNK, A_HEADS, HEAD_DIM)
        ya = jnp.einsum('hts,bnshd->bnthd', ws, vch) + bias_tc
        ya = u * ya.reshape(BATCH, SEQ, A_WIDTH)
        kb = k.reshape(BATCH, n_blocks, BLOCK, KV_HEADS, HEAD_DIM)
        vb = v.reshape(BATCH, n_blocks, BLOCK, KV_HEADS, HEAD_DIM)
        qb = q.reshape(BATCH, n_blocks, BLOCK, N_HEADS, HEAD_DIM)
        yb = attn_core(qb, with_prev_block(kb), with_prev_block(vb), mask_p, sinks[l])
        yb = yb.reshape(BATCH, SEQ, B_WIDTH)
        nk_p.append(k[:, SEQ - WINDOW:])
        nv_p.append(v[:, SEQ - WINDOW:])
        xp = merge_and_ffn(xp, ya, yb, g_out_a[l], g_out_b[l], w_out[l], g_ffn[l], w_up[l], w_down[l])

        u, va, q, k, v = mixer_inputs(xs, pos_s, g_mix[l], w_in[l], g_sv[l], g_q[l], g_k[l])
        ya = jnp.einsum('hts,bshd->bthd', ws[:, :DEC_SEQ, :DEC_SEQ], va) + bias_tc[:DEC_SEQ]
        ya = u * ya.reshape(DEC_BATCH, DEC_SEQ, A_WIDTH)
        nchunk_v.append(va)
        k_ctx = jnp.concatenate([cache_win_k[l].astype(k.dtype), k], axis=1)
        v_ctx = jnp.concatenate([cache_win_v[l].astype(v.dtype), v], axis=1)
        yb = attn_core(q, k_ctx, v_ctx, mask_s, sinks[l]).reshape(DEC_BATCH, DEC_SEQ, B_WIDTH)
        nk_s.append(k_ctx[:, DEC_SEQ:])
        nv_s.append(v_ctx[:, DEC_SEQ:])
        xs = merge_and_ffn(xs, ya, yb, g_out_a[l], g_out_b[l], w_out[l], g_ffn[l], w_up[l], w_down[l])

    state_win_k_prompt = jnp.stack(nk_p)
    state_win_v_prompt = jnp.stack(nv_p)
    state_win_k_sample = jnp.stack(nk_s)
    state_win_v_sample = jnp.stack(nv_s)
    state_chunk_v_sample = jnp.stack(nchunk_v)
    return (xp, xs, state_win_k_prompt, state_win_v_prompt, state_win_k_sample, state_win_v_sample, state_chunk_v_sample)
```

```python
import functools

import jax
import jax.numpy as jnp
import numpy as np
from jax import lax
from jax.experimental import pallas as pl
from jax.experimental.pallas import tpu as pltpu

D_MODEL = 1024
HEAD_DIM = 64
A_WIDTH = 512
B_WIDTH = 512
A_HEADS = A_WIDTH // HEAD_DIM
N_HEADS = B_WIDTH // HEAD_DIM
KV_HEADS = 2
GQA_GROUP = N_HEADS // KV_HEADS
KV_COLS = KV_HEADS * HEAD_DIM
CHUNK = 128
WINDOW = 128
ROPE_THETA = 500000.0
ROT_DIM = HEAD_DIM // 4
D_FF = 4 * D_MODEL
EPS = 1e-6
IN_COLS = 2 * A_WIDTH + B_WIDTH + 2 * KV_COLS
Q_OFF = 2 * A_WIDTH
K_OFF = Q_OFF + B_WIDTH
V_OFF = K_OFF + KV_COLS
NEG_BIG = -1e30

LANES = 128
FF_CHUNK = 1024
PROMPT_TILE = 256
SAMPLE_BATCH_TILE = 16
VMEM_LIMIT_BYTES = 56 * 1024 * 1024

_bf16 = jnp.bfloat16
_f32 = jnp.float32


def _dot(a, b):
    return jnp.dot(a, b, preferred_element_type=_f32)


def _dot_nt(a, b):
    return lax.dot_general(a, b, (((1,), (1,)), ((), ())), preferred_element_type=_f32)


def _rms(x, g):
    return x * lax.rsqrt(jnp.mean(x * x, axis=-1, keepdims=True) + EPS) * g


def _gelu(x):
    return 0.5 * x * (1.0 + lax.erf(x * np.float32(np.sqrt(0.5))))


def _head_rms(z, g_tiled, ones_bd):
    ss = _dot((z * z).astype(_bf16), ones_bd)
    return z * lax.rsqrt(ss * (1.0 / HEAD_DIM) + EPS) * g_tiled


def _rope(x, cos_t, sin_a, sin_b):
    cols = []
    for c in range(x.shape[1] // LANES):
        xc = x[:, c * LANES:(c + 1) * LANES]
        nxt = pltpu.roll(xc, LANES - ROT_DIM // 2, 1)
        prv = pltpu.roll(xc, ROT_DIM // 2, 1)
        cols.append(xc * cos_t + nxt * sin_a + prv * sin_b)
    return cols[0] if len(cols) == 1 else jnp.concatenate(cols, axis=1)


def _front(x, rope_ref, g_mix, w_in, g_sv, g_q, g_k, ones_bd):
    xn = _rms(x, g_mix[...]).astype(_bf16)
    z = _dot(xn, w_in[...])
    u = _gelu(z[:, :A_WIDTH])
    va = _rms(_gelu(z[:, A_WIDTH:2 * A_WIDTH]), g_sv[...])
    cos_t = rope_ref[:, 0:LANES]
    sin_a = rope_ref[:, LANES:2 * LANES]
    sin_b = rope_ref[:, 2 * LANES:3 * LANES]
    ones = ones_bd[...]
    q = _rope(_head_rms(z[:, Q_OFF:K_OFF], g_q[...], ones), cos_t, sin_a, sin_b)
    k = _rope(_head_rms(z[:, K_OFF:V_OFF], g_k[...], ones[:KV_COLS, :KV_COLS]), cos_t, sin_a, sin_b)
    v = z[:, V_OFF:]
    return u, va, q, k, v


def _spatial(va_blk, u_blk, ws_ref, bias):
    vb = va_blk.astype(_bf16)
    outs = [_dot(ws_ref[h], vb[:, h * HEAD_DIM:(h + 1) * HEAD_DIM]) for h in range(A_HEADS)]
    return u_blk * (jnp.concatenate(outs, axis=1) + bias)


def _back(x, ya, yb, g_oa, g_ob, w_out, g_ffn, w_up, w_down):
    cat = jnp.concatenate([_rms(ya, g_oa[...]), _rms(yb, g_ob[...])], axis=1).astype(_bf16)
    x = x + _dot(cat, w_out[...])
    hn = _rms(x, g_ffn[...]).astype(_bf16)
    acc = x
    for c in range(D_FF // FF_CHUNK):
        h = _dot(hn, w_up[:, c * FF_CHUNK:(c + 1) * FF_CHUNK])
        h = jnp.maximum(h, 0.0)
        acc = acc + _dot((h * h).astype(_bf16), w_down[c * FF_CHUNK:(c + 1) * FF_CHUNK, :])
    return acc


def _swap_halves(t):
    return pltpu.roll(t, HEAD_DIM, 1)


def _lane_lo(shape):
    return lax.broadcasted_iota(jnp.int32, shape, len(shape) - 1) < HEAD_DIM


def _prompt_kernel(sinks_ref, x_ref, rope_ref, g_mix, w_in, g_sv, g_q, g_k, ones_bd, ws_ref,
                   bias_ref, g_oa, g_ob, w_out, g_ffn, w_up, w_down,
                   y_ref, ks_ref, vs_ref, kprev, vprev):
    i = pl.program_id(1)
    tile = x_ref.shape[0]
    n_blk = tile // CHUNK

    @pl.when(i == 0)
    def _():
        kprev[...] = jnp.zeros_like(kprev)
        vprev[...] = jnp.zeros_like(vprev)

    x = x_ref[...]
    u, va, q, k, v = _front(x, rope_ref, g_mix, w_in, g_sv, g_q, g_k, ones_bd)
    ks_ref[...] = k[tile - WINDOW:, :]
    vs_ref[...] = v[tile - WINDOW:, :]

    bias = bias_ref[...]
    row = lax.broadcasted_iota(jnp.int32, (CHUNK, 2 * CHUNK), 0)
    col = lax.broadcasted_iota(jnp.int32, (CHUNK, 2 * CHUNK), 1)
    lo = _lane_lo((CHUNK, LANES))
    qs = q * (HEAD_DIM ** -0.5)

    ya_blocks, yb_blocks = [], []
    for blk in range(n_blk):
        r0, r1 = blk * CHUNK, (blk + 1) * CHUNK
        ya_blocks.append(_spatial(va[r0:r1], u[r0:r1], ws_ref, bias))

        k_blk, v_blk = k[r0:r1], v[r0:r1]
        k_new = (k_blk.astype(_bf16), _swap_halves(k_blk).astype(_bf16))
        v_new = (v_blk.astype(_bf16), _swap_halves(v_blk).astype(_bf16))
        if blk == 0:
            k_old = (kprev[0], kprev[1])
            v_old = (vprev[0], vprev[1])
            first = jnp.where(i == 0, 2 * CHUNK, 0)
        else:
            first = 0
        valid = jnp.where(col < CHUNK, col - row - first, row + CHUNK - col) >= 0
        k_ctx = [jnp.concatenate([k_old[s], k_new[s]], axis=0) for s in range(2)]
        v_ctx = [jnp.concatenate([v_old[s], v_new[s]], axis=0) for s in range(2)]
        k_old, v_old = k_new, v_new

        pairs = []
        for p in range(N_HEADS // 2):
            q_pair = qs[r0:r1, p * LANES:(p + 1) * LANES]
            halves = []
            for half in range(2):
                h = 2 * p + half
                kv = h // GQA_GROUP
                src = (kv + half) % 2
                qm = jnp.where(lo if half == 0 else ~lo, q_pair, 0.0).astype(_bf16)
                s = jnp.where(valid, _dot_nt(qm, k_ctx[src]), NEG_BIG)
                sink = sinks_ref[h]
                m = jnp.maximum(jnp.max(s, axis=-1, keepdims=True), sink)
                e = jnp.exp(s - m)
                denom = jnp.sum(e, axis=-1, keepdims=True) + jnp.exp(sink - m)
                o = _dot(e.astype(_bf16), v_ctx[src])
                halves.append(o * (1.0 / denom))
            pairs.append(jnp.where(lo, halves[0], halves[1]))
        yb_blocks.append(jnp.concatenate(pairs, axis=1))

    kprev[0], kprev[1] = k_old
    vprev[0], vprev[1] = v_old
    ya = jnp.concatenate(ya_blocks, axis=0)
    yb = jnp.concatenate(yb_blocks, axis=0)
    y_ref[...] = _back(x, ya, yb, g_oa, g_ob, w_out, g_ffn, w_up, w_down)


def _sample_kernel(x_ref, rope_ref, ck_ref, cv_ref, sink_rows, g_mix, w_in, g_sv, g_q, g_k, ones_bd,
                   ws_ref, bias_ref, g_oa, g_ob, w_out, g_ffn, w_up, w_down,
                   y_ref, ko_ref, vo_ref, va_ref):
    rows = x_ref.shape[0]
    dec_seq = rows // ck_ref.shape[0]
    nb = ck_ref.shape[0]
    x = x_ref[...]
    u, va, q, k, v = _front(x, rope_ref, g_mix, w_in, g_sv, g_q, g_k, ones_bd)
    va_ref[...] = va

    bias = bias_ref[...]
    ya = jnp.concatenate(
        [_spatial(va[r:r + CHUNK], u[r:r + CHUNK], ws_ref, bias) for r in range(0, rows, CHUNK)], axis=0)

    k_ctx = jnp.concatenate([ck_ref[...], k.reshape(nb, dec_seq, KV_COLS)], axis=1)
    v_ctx = jnp.concatenate([cv_ref[...], v.reshape(nb, dec_seq, KV_COLS)], axis=1)
    ko_ref[...] = k_ctx[:, dec_seq:, :]
    vo_ref[...] = v_ctx[:, dec_seq:, :]
    n_ctx = WINDOW + dec_seq

    lo3 = _lane_lo((nb, n_ctx, LANES))
    k_sw, v_sw = pltpu.roll(k_ctx, HEAD_DIM, 2), pltpu.roll(v_ctx, HEAD_DIM, 2)
    k_dup = [jnp.where(lo3, k_ctx, k_sw).astype(_bf16), jnp.where(lo3, k_sw, k_ctx).astype(_bf16)]
    v_dup = [jnp.where(lo3, v_ctx, v_sw).astype(_bf16), jnp.where(lo3, v_sw, v_ctx).astype(_bf16)]

    q3 = (q * (HEAD_DIM ** -0.5)).reshape(nb, dec_seq, B_WIDTH)
    loq = _lane_lo((nb, dec_seq, LANES))
    grp_rows = GQA_GROUP * dec_seq
    qi = lax.broadcasted_iota(jnp.int32, (grp_rows, n_ctx), 0) % dec_seq
    kj = lax.broadcasted_iota(jnp.int32, (grp_rows, n_ctx), 1)
    valid = (jnp.minimum(kj - qi, qi + WINDOW - kj) >= 0)[None]

    pairs = []
    for kv in range(KV_HEADS):
        pieces = []
        for g in range(GQA_GROUP):
            h = kv * GQA_GROUP + g
            q_pair = q3[:, :, (h // 2) * LANES:(h // 2 + 1) * LANES]
            pieces.append(jnp.where(loq if h % 2 == 0 else ~loq, q_pair, 0.0))
        q_st = jnp.concatenate(pieces, axis=1).astype(_bf16)
        s = jnp.einsum('bqd,bkd->bqk', q_st, k_dup[kv], preferred_element_type=_f32)
        s = jnp.where(valid, s, NEG_BIG)
        sink = sink_rows[kv][None]
        m = jnp.maximum(jnp.max(s, axis=-1, keepdims=True), sink)
        e = jnp.exp(s - m)
        denom = jnp.sum(e, axis=-1, keepdims=True) + jnp.exp(sink - m)
        o = jnp.einsum('bqk,bkd->bqd', e.astype(_bf16), v_dup[kv], preferred_element_type=_f32)
        o = o * (1.0 / denom)
        for pp in range(GQA_GROUP // 2):
            o_even = o[:, (2 * pp) * dec_seq:(2 * pp + 1) * dec_seq, :]
            o_odd = o[:, (2 * pp + 1) * dec_seq:(2 * pp + 2) * dec_seq, :]
            pairs.append(jnp.where(loq, o_even, o_odd))
    yb = jnp.concatenate(pairs, axis=2).reshape(rows, B_WIDTH)
    y_ref[...] = _back(x, ya, yb, g_oa, g_ob, w_out, g_ffn, w_up, w_down)


def _const_spec(shape):
    nd = len(shape)
    return pl.BlockSpec(shape, lambda *_: (0,) * nd, pipeline_mode=pl.Buffered(1))


def _rope_tables(pos):
    half = ROT_DIM // 2
    inv = jnp.power(jnp.float32(ROPE_THETA), -2.0 * jnp.arange(half, dtype=_f32) / ROT_DIM)
    ang = pos.astype(_f32)[:, None] * inv[None, :]
    cos, sin = jnp.cos(ang), jnp.sin(ang)
    n = pos.shape[0]
    pad1 = jnp.ones((n, HEAD_DIM - ROT_DIM), _f32)
    pad0 = jnp.zeros((n, HEAD_DIM - ROT_DIM), _f32)
    zero = jnp.zeros((n, half), _f32)
    cos_t = jnp.concatenate([cos, cos, pad1], axis=1)
    sin_a = jnp.concatenate([-sin, zero, pad0], axis=1)
    sin_b = jnp.concatenate([zero, sin, pad0], axis=1)
    rep = LANES // HEAD_DIM
    return jnp.concatenate([jnp.tile(t, (1, rep)) for t in (cos_t, sin_a, sin_b)], axis=1)


def _layer_weights(l, g_mix, w_in, g_sv, g_q, g_k, ones_bd, g_out_a, g_out_b, w_out, g_ffn, w_up, w_down):
    head = (g_mix[l][None], w_in[l].astype(_bf16), g_sv[l][None],
            jnp.tile(g_q[l], N_HEADS)[None], jnp.tile(g_k[l], KV_HEADS)[None], ones_bd)
    tail = (g_out_a[l][None], g_out_b[l][None], w_out[l].astype(_bf16), g_ffn[l][None],
            w_up[l].astype(_bf16), w_down[l].astype(_bf16))
    return head, tail


def kernel(x_prompt, x_sample, cache_win_k, cache_win_v, g_mix, w_in, g_sv, g_q, g_k, w_spatial,
           b_spatial, sinks, g_out_a, g_out_b, w_out, g_ffn, w_up, w_down):
    batch, seq, _ = x_prompt.shape
    dec_batch, dec_seq, _ = x_sample.shape
    depth = w_in.shape[0]
    assert seq % PROMPT_TILE == 0 and PROMPT_TILE % CHUNK == 0
    assert dec_batch % SAMPLE_BATCH_TILE == 0 and CHUNK % dec_seq == 0
    s_rows = SAMPLE_BATCH_TILE * dec_seq
    assert s_rows % CHUNK == 0

    rope_p = _rope_tables(jnp.arange(seq, dtype=jnp.int32))
    rope_s = jnp.tile(_rope_tables(seq + jnp.arange(dec_seq, dtype=jnp.int32)), (SAMPLE_BATCH_TILE, 1))
    hid = np.arange(B_WIDTH) // HEAD_DIM
    ones_bd = jnp.asarray(hid[:, None] == hid[None, :], dtype=_bf16)
    tril = jnp.tril(jnp.ones((CHUNK, CHUNK), dtype=bool))
    eye_rep = jnp.eye(CHUNK // dec_seq, dtype=_f32)

    weight_shapes_head = [(1, D_MODEL), (D_MODEL, IN_COLS), (1, A_WIDTH), (1, B_WIDTH), (1, KV_COLS),
                          (B_WIDTH, B_WIDTH)]
    weight_shapes_tail = [(1, A_WIDTH), (1, B_WIDTH), (D_MODEL, D_MODEL), (1, D_MODEL),
                          (D_MODEL, D_FF), (D_FF, D_MODEL)]
    spatial_shapes = [(A_HEADS, CHUNK, CHUNK), (CHUNK, A_WIDTH)]
    cparams = functools.partial(pltpu.CompilerParams, vmem_limit_bytes=VMEM_LIMIT_BYTES)

    prompt_call = pl.pallas_call(
        _prompt_kernel,
        grid=(batch, seq // PROMPT_TILE),
        in_specs=[pl.BlockSpec(memory_space=pltpu.SMEM),
                  pl.BlockSpec((None, PROMPT_TILE, D_MODEL), lambda b, i: (b, i, 0)),
                  pl.BlockSpec((PROMPT_TILE, 3 * LANES), lambda b, i: (i, 0))]
                 + [_const_spec(s) for s in weight_shapes_head + spatial_shapes + weight_shapes_tail],
        out_specs=[pl.BlockSpec((None, PROMPT_TILE, D_MODEL), lambda b, i: (b, i, 0)),
                   pl.BlockSpec((None, WINDOW, KV_COLS), lambda b, i: (b, 0, 0)),
                   pl.BlockSpec((None, WINDOW, KV_COLS), lambda b, i: (b, 0, 0))],
        out_shape=[jax.ShapeDtypeStruct((batch, seq, D_MODEL), _f32),
                   jax.ShapeDtypeStruct((batch, WINDOW, KV_COLS), _f32),
                   jax.ShapeDtypeStruct((batch, WINDOW, KV_COLS), _f32)],
        scratch_shapes=[pltpu.VMEM((2, CHUNK, KV_COLS), _bf16), pltpu.VMEM((2, CHUNK, KV_COLS), _bf16)],
        compiler_params=cparams(dimension_semantics=("arbitrary", "arbitrary")),
        name="prompt_layer",
    )

    sample_call = pl.pallas_call(
        _sample_kernel,
        grid=(dec_batch // SAMPLE_BATCH_TILE,),
        in_specs=[pl.BlockSpec((s_rows, D_MODEL), lambda i: (i, 0)),
                  _const_spec((s_rows, 3 * LANES)),
                  pl.BlockSpec((SAMPLE_BATCH_TILE, WINDOW, KV_COLS), lambda i: (i, 0, 0)),
                  pl.BlockSpec((SAMPLE_BATCH_TILE, WINDOW, KV_COLS), lambda i: (i, 0, 0)),
                  _const_spec((KV_HEADS, GQA_GROUP * dec_seq, 1))]
                 + [_const_spec(s) for s in weight_shapes_head + spatial_shapes + weight_shapes_tail],
        out_specs=[pl.BlockSpec((s_rows, D_MODEL), lambda i: (i, 0)),
                   pl.BlockSpec((SAMPLE_BATCH_TILE, WINDOW, KV_COLS), lambda i: (i, 0, 0)),
                   pl.BlockSpec((SAMPLE_BATCH_TILE, WINDOW, KV_COLS), lambda i: (i, 0, 0)),
                   pl.BlockSpec((s_rows, A_WIDTH), lambda i: (i, 0))],
        out_shape=[jax.ShapeDtypeStruct((dec_batch * dec_seq, D_MODEL), _f32),
                   jax.ShapeDtypeStruct((dec_batch, WINDOW, KV_COLS), _f32),
                   jax.ShapeDtypeStruct((dec_batch, WINDOW, KV_COLS), _f32),
                   jax.ShapeDtypeStruct((dec_batch * dec_seq, A_WIDTH), _f32)],
        compiler_params=cparams(dimension_semantics=("arbitrary",)),
        name="sample_layer",
    )

    xp = x_prompt
    xs = x_sample.reshape(dec_batch * dec_seq, D_MODEL)
    ck = cache_win_k.reshape(depth, dec_batch, WINDOW, KV_COLS)
    cv = cache_win_v.reshape(depth, dec_batch, WINDOW, KV_COLS)
    nk_p, nv_p, nk_s, nv_s, nchunk_v = [], [], [], [], []
    for l in range(depth):
        head, tail = _layer_weights(l, g_mix, w_in, g_sv, g_q, g_k, ones_bd, g_out_a, g_out_b,
                                    w_out, g_ffn, w_up, w_down)
        ws = jnp.where(tril, w_spatial[l], 0.0)
        bias = jnp.repeat(b_spatial[l].T, HEAD_DIM, axis=1)
        ws_s = jax.vmap(lambda w: jnp.kron(eye_rep, w))(ws[:, :dec_seq, :dec_seq])
        bias_s = jnp.tile(bias[:dec_seq], (CHUNK // dec_seq, 1))
        sink_rows = jnp.repeat(sinks[l].reshape(KV_HEADS, GQA_GROUP), dec_seq, axis=1)[..., None]

        xp, kp, vp = prompt_call(sinks[l], xp, rope_p, *head, ws.astype(_bf16), bias, *tail)
        nk_p.append(kp)
        nv_p.append(vp)
        xs, ks, vs, va_s = sample_call(xs, rope_s, ck[l], cv[l], sink_rows, *head,
                                       ws_s.astype(_bf16), bias_s, *tail)
        nk_s.append(ks)
        nv_s.append(vs)
        nchunk_v.append(va_s)

    kv5 = lambda ts, n: jnp.stack(ts).reshape(depth, n, WINDOW, KV_HEADS, HEAD_DIM)
    return (xp, xs.reshape(dec_batch, dec_seq, D_MODEL),
            kv5(nk_p, batch), kv5(nv_p, batch), kv5(nk_s, dec_batch), kv5(nv_s, dec_batch),
            jnp.stack(nchunk_v).reshape(depth, dec_batch, dec_seq, A_HEADS, HEAD_DIM))
```

```python
import functools

import jax
import jax.numpy as jnp
import numpy as np
from jax import lax
from jax.experimental import pallas as pl
from jax.experimental.pallas import tpu as pltpu

D_MODEL = 1024
HEAD_DIM = 64
A_WIDTH = 512
B_WIDTH = 512
A_HEADS = A_WIDTH // HEAD_DIM
N_HEADS = B_WIDTH // HEAD_DIM
KV_HEADS = 2
GQA_GROUP = N_HEADS // KV_HEADS
KV_COLS = KV_HEADS * HEAD_DIM
CHUNK = 128
WINDOW = 128
ROPE_THETA = 500000.0
ROT_DIM = HEAD_DIM // 4
D_FF = 4 * D_MODEL
EPS = 1e-6
IN_COLS = 2 * A_WIDTH + B_WIDTH + 2 * KV_COLS
Q_OFF = 2 * A_WIDTH
K_OFF = Q_OFF + B_WIDTH
V_OFF = K_OFF + KV_COLS
NEG_BIG = -1e30

LANES = 128
FF_CHUNK = 1024
PROMPT_TILE = 256
SAMPLE_BATCH_TILE = 16
VMEM_LIMIT_BYTES = 56 * 1024 * 1024

_bf16 = jnp.bfloat16
_f32 = jnp.float32


def _dot(a, b):
    return jnp.dot(a, b, preferred_element_type=_f32)


def _dot_nt(a, b):
    return lax.dot_general(a, b, (((1,), (1,)), ((), ())), preferred_element_type=_f32)


def _rms(x, g):
    return x * lax.rsqrt(jnp.mean(x * x, axis=-1, keepdims=True) + EPS) * g


def _gelu(x):
    return 0.5 * x * (1.0 + lax.erf(x * np.float32(np.sqrt(0.5))))


def _head_rms(z, g_tiled, ones_bd):
    ss = _dot((z * z).astype(_bf16), ones_bd)
    return z * lax.rsqrt(ss * (1.0 / HEAD_DIM) + EPS) * g_tiled


def _rope(x, cos_t, sin_a, sin_b):
    cols = []
    for c in range(x.shape[1] // LANES):
        xc = x[:, c * LANES:(c + 1) * LANES]
        nxt = pltpu.roll(xc, LANES - ROT_DIM // 2, 1)
        prv = pltpu.roll(xc, ROT_DIM // 2, 1)
        cols.append(xc * cos_t + nxt * sin_a + prv * sin_b)
    return cols[0] if len(cols) == 1 else jnp.concatenate(cols, axis=1)


def _in_proj(x, g_mix, w_in):
    return _dot(_rms(x, g_mix[...]).astype(_bf16), w_in[...])


def _split_inputs(z, rope_ref, g_sv, g_q, g_k, ones_bd):
    u = _gelu(z[:, :A_WIDTH])
    va = _rms(_gelu(z[:, A_WIDTH:2 * A_WIDTH]), g_sv[...])
    cos_t = rope_ref[:, 0:LANES]
    sin_a = rope_ref[:, LANES:2 * LANES]
    sin_b = rope_ref[:, 2 * LANES:3 * LANES]
    ones = ones_bd[...]
    q = _rope(_head_rms(z[:, Q_OFF:K_OFF], g_q[...], ones), cos_t, sin_a, sin_b)
    k = _rope(_head_rms(z[:, K_OFF:V_OFF], g_k[...], ones[:KV_COLS, :KV_COLS]), cos_t, sin_a, sin_b)
    v = z[:, V_OFF:]
    return u, va, q, k, v


def _spatial(va_blk, u_blk, ws_ref, bias):
    vb = va_blk.astype(_bf16)
    outs = [_dot(ws_ref[h], vb[:, h * HEAD_DIM:(h + 1) * HEAD_DIM]) for h in range(A_HEADS)]
    return u_blk * (jnp.concatenate(outs, axis=1) + bias)


def _merge(x, ya, yb, g_oa, g_ob, w_out, g_ffn):
    cat = jnp.concatenate([_rms(ya, g_oa[...]), _rms(yb, g_ob[...])], axis=1).astype(_bf16)
    x = x + _dot(cat, w_out[...])
    return x, _rms(x, g_ffn[...]).astype(_bf16)


N_FF_CHUNKS = D_FF // FF_CHUNK


def _ffn_chunk(acc, hn, w_up, w_down, c):
    h = _dot(hn, w_up[:, c * FF_CHUNK:(c + 1) * FF_CHUNK])
    h = jnp.maximum(h, 0.0)
    return acc + _dot((h * h).astype(_bf16), w_down[c * FF_CHUNK:(c + 1) * FF_CHUNK, :])


def _ffn(x, hn, w_up, w_down):
    acc = x
    for c in range(N_FF_CHUNKS):
        acc = _ffn_chunk(acc, hn, w_up, w_down, c)
    return acc


def _swap_halves(t):
    return pltpu.roll(t, HEAD_DIM, 1)


def _lane_lo(shape):
    return lax.broadcasted_iota(jnp.int32, shape, len(shape) - 1) < HEAD_DIM


def _prompt_kernel(sinks_ref, x_ref, rope_ref, g_mix, w_in, g_sv, g_q, g_k, ones_bd, ws_ref,
                   bias_ref, g_oa, g_ob, w_out, g_ffn, w_up, w_down,
                   y_ref, ks_ref, vs_ref, kprev, vprev, x_mid, h_mid, *, n_tiles, total):
    s = pl.program_id(0)
    i = jnp.minimum(s, total - 1) % n_tiles
    tile = x_ref.shape[0]
    n_blk = tile // CHUNK

    @pl.when(s == 0)
    def _():
        x_mid[...] = jnp.zeros_like(x_mid)
        h_mid[...] = jnp.zeros_like(h_mid)

    @pl.when(i == 0)
    def _():
        kprev[...] = jnp.zeros_like(kprev)
        vprev[...] = jnp.zeros_like(vprev)

    hn_prev = h_mid[...]
    mlp = {"acc": x_mid[...], "done": 0}
    n_gaps = n_blk + 2

    def mlp_chunks(gap):
        upto = ((gap + 1) * N_FF_CHUNKS) // n_gaps
        for c in range(mlp["done"], upto):
            mlp["acc"] = _ffn_chunk(mlp["acc"], hn_prev, w_up, w_down, c)
        mlp["done"] = upto

    x = x_ref[...]
    z = _in_proj(x, g_mix, w_in)
    mlp_chunks(0)
    u, va, q, k, v = _split_inputs(z, rope_ref, g_sv, g_q, g_k, ones_bd)
    ks_ref[...] = k[tile - WINDOW:, :]
    vs_ref[...] = v[tile - WINDOW:, :]
    mlp_chunks(1)

    bias = bias_ref[...]
    row = lax.broadcasted_iota(jnp.int32, (CHUNK, 2 * CHUNK), 0)
    col = lax.broadcasted_iota(jnp.int32, (CHUNK, 2 * CHUNK), 1)
    lo = _lane_lo((CHUNK, LANES))
    qs = q * (HEAD_DIM ** -0.5)

    ya_blocks, yb_blocks = [], []
    for blk in range(n_blk):
        if blk:
            mlp_chunks(1 + blk)
        r0, r1 = blk * CHUNK, (blk + 1) * CHUNK
        ya_blocks.append(_spatial(va[r0:r1], u[r0:r1], ws_ref, bias))

        k_blk, v_blk = k[r0:r1], v[r0:r1]
        k_new = (k_blk.astype(_bf16), _swap_halves(k_blk).astype(_bf16))
        v_new = (v_blk.astype(_bf16), _swap_halves(v_blk).astype(_bf16))
        if blk == 0:
            k_old = (kprev[0], kprev[1])
            v_old = (vprev[0], vprev[1])
            first = jnp.where(i == 0, 2 * CHUNK, 0)
        else:
            first = 0
        valid = jnp.where(col < CHUNK, col - row - first, row + CHUNK - col) >= 0
        k_ctx = [jnp.concatenate([k_old[a], k_new[a]], axis=0) for a in range(2)]
        v_ctx = [jnp.concatenate([v_old[a], v_new[a]], axis=0) for a in range(2)]
        k_old, v_old = k_new, v_new

        pairs = []
        for p in range(N_HEADS // 2):
            q_pair = qs[r0:r1, p * LANES:(p + 1) * LANES]
            halves = []
            for half in range(2):
                h = 2 * p + half
                kv = h // GQA_GROUP
                src = (kv + half) % 2
                qm = jnp.where(lo if half == 0 else ~lo, q_pair, 0.0).astype(_bf16)
                sc = jnp.where(valid, _dot_nt(qm, k_ctx[src]), NEG_BIG)
                sink = sinks_ref[h]
                m = jnp.maximum(jnp.max(sc, axis=-1, keepdims=True), sink)
                e = jnp.exp(sc - m)
                denom = jnp.sum(e, axis=-1, keepdims=True) + jnp.exp(sink - m)
                o = _dot(e.astype(_bf16), v_ctx[src])
                halves.append(o * (1.0 / denom))
            pairs.append(jnp.where(lo, halves[0], halves[1]))
        yb_blocks.append(jnp.concatenate(pairs, axis=1))

    mlp_chunks(n_gaps - 1)
    y_ref[...] = mlp["acc"]
    kprev[0], kprev[1] = k_old
    vprev[0], vprev[1] = v_old
    ya = jnp.concatenate(ya_blocks, axis=0)
    yb = jnp.concatenate(yb_blocks, axis=0)
    x_new, hn = _merge(x, ya, yb, g_oa, g_ob, w_out, g_ffn)
    x_mid[...] = x_new
    h_mid[...] = hn


def _sample_kernel(x_ref, rope_ref, ck_ref, cv_ref, sink_rows, g_mix, w_in, g_sv, g_q, g_k, ones_bd,
                   ws_ref, bias_ref, g_oa, g_ob, w_out, g_ffn, w_up, w_down,
                   y_ref, ko_ref, vo_ref, va_ref, x_all):
    l = pl.program_id(0)
    i = pl.program_id(1)
    rows = x_ref.shape[0]
    nb = ck_ref.shape[0]
    dec_seq = rows // nb

    @pl.when(l == 0)
    def _():
        x_all[i] = x_ref[...]

    x = x_all[i]
    u, va, q, k, v = _split_inputs(_in_proj(x, g_mix, w_in), rope_ref, g_sv, g_q, g_k, ones_bd)
    va_ref[...] = va

    bias = bias_ref[...]
    ya = jnp.concatenate(
        [_spatial(va[r:r + CHUNK], u[r:r + CHUNK], ws_ref, bias) for r in range(0, rows, CHUNK)], axis=0)

    k_ctx = jnp.concatenate([ck_ref[...], k.reshape(nb, dec_seq, KV_COLS)], axis=1)
    v_ctx = jnp.concatenate([cv_ref[...], v.reshape(nb, dec_seq, KV_COLS)], axis=1)
    ko_ref[...] = k_ctx[:, dec_seq:, :]
    vo_ref[...] = v_ctx[:, dec_seq:, :]
    n_ctx = WINDOW + dec_seq

    lo3 = _lane_lo((nb, n_ctx, LANES))
    k_sw, v_sw = pltpu.roll(k_ctx, HEAD_DIM, 2), pltpu.roll(v_ctx, HEAD_DIM, 2)
    k_dup = [jnp.where(lo3, k_ctx, k_sw).astype(_bf16), jnp.where(lo3, k_sw, k_ctx).astype(_bf16)]
    v_dup = [jnp.where(lo3, v_ctx, v_sw).astype(_bf16), jnp.where(lo3, v_sw, v_ctx).astype(_bf16)]

    q3 = (q * (HEAD_DIM ** -0.5)).reshape(nb, dec_seq, B_WIDTH)
    loq = _lane_lo((nb, dec_seq, LANES))
    grp_rows = GQA_GROUP * dec_seq
    qi = lax.broadcasted_iota(jnp.int32, (grp_rows, n_ctx), 0) % dec_seq
    kj = lax.broadcasted_iota(jnp.int32, (grp_rows, n_ctx), 1)
    valid = (jnp.minimum(kj - qi, qi + WINDOW - kj) >= 0)[None]

    pairs = []
    for kv in range(KV_HEADS):
        pieces = []
        for g in range(GQA_GROUP):
            h = kv * GQA_GROUP + g
            q_pair = q3[:, :, (h // 2) * LANES:(h // 2 + 1) * LANES]
            pieces.append(jnp.where(loq if h % 2 == 0 else ~loq, q_pair, 0.0))
        q_st = jnp.concatenate(pieces, axis=1).astype(_bf16)
        sc = jnp.einsum('bqd,bkd->bqk', q_st, k_dup[kv], preferred_element_type=_f32)
        sc = jnp.where(valid, sc, NEG_BIG)
        sink = sink_rows[kv][None]
        m = jnp.maximum(jnp.max(sc, axis=-1, keepdims=True), sink)
        e = jnp.exp(sc - m)
        denom = jnp.sum(e, axis=-1, keepdims=True) + jnp.exp(sink - m)
        o = jnp.einsum('bqk,bkd->bqd', e.astype(_bf16), v_dup[kv], preferred_element_type=_f32)
        o = o * (1.0 / denom)
        for pp in range(GQA_GROUP // 2):
            o_even = o[:, (2 * pp) * dec_seq:(2 * pp + 1) * dec_seq, :]
            o_odd = o[:, (2 * pp + 1) * dec_seq:(2 * pp + 2) * dec_seq, :]
            pairs.append(jnp.where(loq, o_even, o_odd))
    yb = jnp.concatenate(pairs, axis=2).reshape(rows, B_WIDTH)
    x_new, hn = _merge(x, ya, yb, g_oa, g_ob, w_out, g_ffn)
    y = _ffn(x_new, hn, w_up, w_down)
    x_all[i] = y
    y_ref[...] = y


def _rope_tables(pos):
    half = ROT_DIM // 2
    inv = np.power(np.float32(ROPE_THETA), -2.0 * np.arange(half, dtype=np.float32) / np.float32(ROT_DIM))
    ang = pos.astype(np.float32)[:, None] * inv.astype(np.float32)[None, :]
    cos, sin = np.cos(ang).astype(np.float32), np.sin(ang).astype(np.float32)
    n = pos.shape[0]
    pad1 = np.ones((n, HEAD_DIM - ROT_DIM), np.float32)
    pad0 = np.zeros((n, HEAD_DIM - ROT_DIM), np.float32)
    zero = np.zeros((n, half), np.float32)
    cos_t = np.concatenate([cos, cos, pad1], axis=1)
    sin_a = np.concatenate([-sin, zero, pad0], axis=1)
    sin_b = np.concatenate([zero, sin, pad0], axis=1)
    rep = LANES // HEAD_DIM
    return np.concatenate([np.tile(t, (1, rep)) for t in (cos_t, sin_a, sin_b)], axis=1)


def kernel(x_prompt, x_sample, cache_win_k, cache_win_v, g_mix, w_in, g_sv, g_q, g_k, w_spatial,
           b_spatial, sinks, g_out_a, g_out_b, w_out, g_ffn, w_up, w_down):
    batch, seq, _ = x_prompt.shape
    dec_batch, dec_seq, _ = x_sample.shape
    depth = w_in.shape[0]
    assert seq % PROMPT_TILE == 0 and PROMPT_TILE % CHUNK == 0
    assert dec_batch % SAMPLE_BATCH_TILE == 0 and CHUNK % dec_seq == 0
    s_rows = SAMPLE_BATCH_TILE * dec_seq
    assert s_rows % CHUNK == 0
    n_tiles = seq // PROMPT_TILE
    total = batch * n_tiles
    s_tiles = dec_batch // SAMPLE_BATCH_TILE

    rope_p = jnp.asarray(_rope_tables(np.arange(seq)))
    rope_s = jnp.asarray(np.tile(_rope_tables(seq + np.arange(dec_seq)), (SAMPLE_BATCH_TILE, 1)))
    hid = np.arange(B_WIDTH) // HEAD_DIM
    ones_bd = jnp.asarray(hid[:, None] == hid[None, :], dtype=_bf16)
    tril = np.tril(np.ones((CHUNK, CHUNK), dtype=bool))
    rep = CHUNK // dec_seq
    expand = jnp.asarray(np.tile(np.eye(dec_seq, dtype=np.float32), (rep, 1)))
    same_batch = jnp.asarray(np.kron(np.eye(rep), np.ones((dec_seq, dec_seq))), dtype=_f32)

    row3 = lambda a: a[:, None, :]
    ws = jnp.where(tril, w_spatial, 0.0)
    bias = jnp.repeat(jnp.swapaxes(b_spatial, 1, 2), HEAD_DIM, axis=2)
    ws_s = jnp.einsum('rt,lhts,cs->lhrc', expand, ws[:, :, :dec_seq, :dec_seq], expand,
                      precision=lax.Precision.HIGHEST) * same_batch
    bias_s = jnp.tile(bias[:, :dec_seq], (1, rep, 1))
    sink_rows = jnp.repeat(sinks.reshape(depth, KV_HEADS, GQA_GROUP), dec_seq, axis=2)[..., None]
    head = (row3(g_mix), w_in.astype(_bf16), row3(g_sv), row3(jnp.tile(g_q, (1, N_HEADS))),
            row3(jnp.tile(g_k, (1, KV_HEADS))))
    tail = (row3(g_out_a), row3(g_out_b), w_out.astype(_bf16), row3(g_ffn),
            w_up.astype(_bf16), w_down.astype(_bf16))
    head_shapes = [(1, D_MODEL), (D_MODEL, IN_COLS), (1, A_WIDTH), (1, B_WIDTH), (1, KV_COLS)]
    tail_shapes = [(1, A_WIDTH), (1, B_WIDTH), (D_MODEL, D_MODEL), (1, D_MODEL), (D_MODEL, D_FF), (D_FF, D_MODEL)]
    spatial_shapes = [(A_HEADS, CHUNK, CHUNK), (CHUNK, A_WIDTH)]

    def const_spec(shape):
        nd = len(shape)
        return pl.BlockSpec(shape, lambda *_: (0,) * nd, pipeline_mode=pl.Buffered(1))

    def layer_spec(shape, layer_of):
        nd = len(shape)
        return pl.BlockSpec((None,) + tuple(shape), lambda *g: (layer_of(*g),) + (0,) * nd,
                            pipeline_mode=pl.Buffered(1))

    def weight_specs(layer_of):
        return ([layer_spec(sh, layer_of) for sh in head_shapes] + [const_spec((B_WIDTH, B_WIDTH))]
                + [layer_spec(sh, layer_of) for sh in spatial_shapes + tail_shapes])

    cparams = functools.partial(pltpu.CompilerParams, vmem_limit_bytes=VMEM_LIMIT_BYTES)

    def prompt_layer(l, xp):
        tile_of = lambda s: jnp.minimum(s, total - 1)
        return pl.pallas_call(
            functools.partial(_prompt_kernel, n_tiles=n_tiles, total=total),
            grid=(total + 1,),
            in_specs=[pl.BlockSpec(memory_space=pltpu.SMEM),
                      pl.BlockSpec((PROMPT_TILE, D_MODEL), lambda s: (tile_of(s), 0)),
                      pl.BlockSpec((PROMPT_TILE, 3 * LANES), lambda s: (tile_of(s) % n_tiles, 0))]
                     + weight_specs(lambda s: l),
            out_specs=[pl.BlockSpec((PROMPT_TILE, D_MODEL), lambda s: (jnp.maximum(s - 1, 0), 0)),
                       pl.BlockSpec((None, WINDOW, KV_COLS), lambda s: (tile_of(s) // n_tiles, 0, 0)),
                       pl.BlockSpec((None, WINDOW, KV_COLS), lambda s: (tile_of(s) // n_tiles, 0, 0))],
            out_shape=[jax.ShapeDtypeStruct((batch * seq, D_MODEL), _f32),
                       jax.ShapeDtypeStruct((batch, WINDOW, KV_COLS), _f32),
                       jax.ShapeDtypeStruct((batch, WINDOW, KV_COLS), _f32)],
            scratch_shapes=[pltpu.VMEM((2, CHUNK, KV_COLS), _bf16), pltpu.VMEM((2, CHUNK, KV_COLS), _bf16),
                            pltpu.VMEM((PROMPT_TILE, D_MODEL), _f32), pltpu.VMEM((PROMPT_TILE, D_MODEL), _bf16)],
            compiler_params=cparams(dimension_semantics=("arbitrary",)),
            name="prompt_layer",
        )(sinks[l], xp, rope_p, *head, ones_bd, ws.astype(_bf16), bias, *tail)

    xp = x_prompt.reshape(batch * seq, D_MODEL)
    nk_p, nv_p = [], []
    for l in range(depth):
        xp, kp, vp = prompt_layer(l, xp)
        nk_p.append(kp)
        nv_p.append(vp)

    layer_tile = lambda l, i: (l, i, 0, 0)
    ys, ks, vs, va_s = pl.pallas_call(
        _sample_kernel,
        grid=(depth, s_tiles),
        in_specs=[pl.BlockSpec((s_rows, D_MODEL), lambda l, i: (jnp.where(l == 0, i, s_tiles - 1), 0)),
                  const_spec((s_rows, 3 * LANES)),
                  pl.BlockSpec((None, SAMPLE_BATCH_TILE, WINDOW, KV_COLS), layer_tile),
                  pl.BlockSpec((None, SAMPLE_BATCH_TILE, WINDOW, KV_COLS), layer_tile),
                  layer_spec((KV_HEADS, GQA_GROUP * dec_seq, 1), lambda l, i: l)]
                 + weight_specs(lambda l, i: l),
        out_specs=[pl.BlockSpec((s_rows, D_MODEL), lambda l, i: (jnp.where(l == depth - 1, i, 0), 0)),
                   pl.BlockSpec((None, SAMPLE_BATCH_TILE, WINDOW, KV_COLS), layer_tile),
                   pl.BlockSpec((None, SAMPLE_BATCH_TILE, WINDOW, KV_COLS), layer_tile),
                   pl.BlockSpec((None, s_rows, A_WIDTH), lambda l, i: (l, i, 0))],
        out_shape=[jax.ShapeDtypeStruct((dec_batch * dec_seq, D_MODEL), _f32),
                   jax.ShapeDtypeStruct((depth, dec_batch, WINDOW, KV_COLS), _f32),
                   jax.ShapeDtypeStruct((depth, dec_batch, WINDOW, KV_COLS), _f32),
                   jax.ShapeDtypeStruct((depth, dec_batch * dec_seq, A_WIDTH), _f32)],
        scratch_shapes=[pltpu.VMEM((s_tiles, s_rows, D_MODEL), _f32)],
        compiler_params=cparams(dimension_semantics=("arbitrary", "arbitrary")),
        name="sample_layers",
    )(x_sample.reshape(dec_batch * dec_seq, D_MODEL), rope_s,
      cache_win_k.reshape(depth, dec_batch, WINDOW, KV_COLS),
      cache_win_v.reshape(depth, dec_batch, WINDOW, KV_COLS), sink_rows,
      *head, ones_bd, ws_s.astype(_bf16), bias_s, *tail)

    kv5 = lambda t, n: t.reshape(depth, n, WINDOW, KV_HEADS, HEAD_DIM)
    return (xp.reshape(batch, seq, D_MODEL), ys.reshape(dec_batch, dec_seq, D_MODEL),
            kv5(jnp.stack(nk_p), batch), kv5(jnp.stack(nv_p), batch), kv5(ks, dec_batch), kv5(vs, dec_batch),
            va_s.reshape(depth, dec_batch, dec_seq, A_HEADS, HEAD_DIM))
```

```python
import functools

import jax
import jax.numpy as jnp
import numpy as np
from jax import lax
from jax.experimental import pallas as pl
from jax.experimental.pallas import tpu as pltpu

D_MODEL = 1024
HEAD_DIM = 64
A_WIDTH = 512
B_WIDTH = 512
A_HEADS = A_WIDTH // HEAD_DIM
N_HEADS = B_WIDTH // HEAD_DIM
KV_HEADS = 2
GQA_GROUP = N_HEADS // KV_HEADS
KV_COLS = KV_HEADS * HEAD_DIM
CHUNK = 128
WINDOW = 128
ROPE_THETA = 500000.0
ROT_DIM = HEAD_DIM // 4
D_FF = 4 * D_MODEL
EPS = 1e-6
IN_COLS = 2 * A_WIDTH + B_WIDTH + 2 * KV_COLS
Q_OFF = 2 * A_WIDTH
K_OFF = Q_OFF + B_WIDTH
V_OFF = K_OFF + KV_COLS
NEG_BIG = -1e30

LANES = 128
FF_CHUNK = 1024
PROMPT_TILE = 512
SAMPLE_BATCH_TILE = 16
VMEM_LIMIT_BYTES = 56 * 1024 * 1024

_bf16 = jnp.bfloat16
_f32 = jnp.float32


def _dot(a, b):
    return jnp.dot(a, b, preferred_element_type=_f32)


def _dot_nt(a, b):
    return lax.dot_general(a, b, (((1,), (1,)), ((), ())), preferred_element_type=_f32)


def _rms(x, g):
    return x * lax.rsqrt(jnp.mean(x * x, axis=-1, keepdims=True) + EPS) * g


def _gelu(x):
    return 0.5 * x * (1.0 + lax.erf(x * np.float32(np.sqrt(0.5))))


def _head_rms(z, g_tiled, ones_bd):
    ss = _dot((z * z).astype(_bf16), ones_bd)
    return z * lax.rsqrt(ss * (1.0 / HEAD_DIM) + EPS) * g_tiled


def _rope(x, cos_t, sin_a, sin_b):
    cols = []
    for c in range(x.shape[1] // LANES):
        xc = x[:, c * LANES:(c + 1) * LANES]
        nxt = pltpu.roll(xc, LANES - ROT_DIM // 2, 1)
        prv = pltpu.roll(xc, ROT_DIM // 2, 1)
        cols.append(xc * cos_t + nxt * sin_a + prv * sin_b)
    return cols[0] if len(cols) == 1 else jnp.concatenate(cols, axis=1)


def _in_proj(x, g_mix, w_in):
    return _dot(_rms(x, g_mix[...]).astype(_bf16), w_in[...])


def _split_inputs(z, rope_ref, g_sv, g_q, g_k, ones_bd):
    u = _gelu(z[:, :A_WIDTH])
    va = _rms(_gelu(z[:, A_WIDTH:2 * A_WIDTH]), g_sv[...])
    cos_t = rope_ref[:, 0:LANES]
    sin_a = rope_ref[:, LANES:2 * LANES]
    sin_b = rope_ref[:, 2 * LANES:3 * LANES]
    ones = ones_bd[...]
    q = _rope(_head_rms(z[:, Q_OFF:K_OFF], g_q[...], ones), cos_t, sin_a, sin_b)
    k = _rope(_head_rms(z[:, K_OFF:V_OFF], g_k[...], ones[:KV_COLS, :KV_COLS]), cos_t, sin_a, sin_b)
    v = z[:, V_OFF:]
    return u, va, q, k, v


def _spatial(va_blk, u_blk, ws_ref, bias):
    vb = va_blk.astype(_bf16)
    outs = [_dot(ws_ref[h], vb[:, h * HEAD_DIM:(h + 1) * HEAD_DIM]) for h in range(A_HEADS)]
    return u_blk * (jnp.concatenate(outs, axis=1) + bias)


def _merge(x, ya, yb, g_oa, g_ob, w_out, g_ffn):
    cat = jnp.concatenate([_rms(ya, g_oa[...]), _rms(yb, g_ob[...])], axis=1).astype(_bf16)
    x = x + _dot(cat, w_out[...])
    return x, _rms(x, g_ffn[...]).astype(_bf16)


def _ffn_chunk(acc, hn, w_up, w_down, c, width):
    h = _dot(hn, w_up[:, c * width:(c + 1) * width])
    h = jnp.maximum(h, 0.0)
    return acc + _dot((h * h).astype(_bf16), w_down[c * width:(c + 1) * width, :])


def _ffn(x, hn, w_up, w_down):
    acc = x
    for c in range(D_FF // FF_CHUNK):
        acc = _ffn_chunk(acc, hn, w_up, w_down, c, FF_CHUNK)
    return acc


def _swap_halves(t):
    return pltpu.roll(t, HEAD_DIM, 1)


def _lane_lo(shape):
    return lax.broadcasted_iota(jnp.int32, shape, len(shape) - 1) < HEAD_DIM


def _prompt_kernel(sinks_ref, x_ref, rope_ref, g_mix, w_in, g_sv, g_q, g_k, ones_bd, ws_ref,
                   bias_ref, g_oa, g_ob, w_out, g_ffn, w_up, w_down,
                   y_ref, ks_ref, vs_ref, kprev, vprev):
    i = pl.program_id(1)
    tile = x_ref.shape[0]
    n_blk = tile // CHUNK

    @pl.when(i == 0)
    def _():
        kprev[...] = jnp.zeros_like(kprev)
        vprev[...] = jnp.zeros_like(vprev)

    x = x_ref[...]
    u, va, q, k, v = _split_inputs(_in_proj(x, g_mix, w_in), rope_ref, g_sv, g_q, g_k, ones_bd)
    ks_ref[...] = k[tile - WINDOW:, :]
    vs_ref[...] = v[tile - WINDOW:, :]

    bias = bias_ref[...]
    row = lax.broadcasted_iota(jnp.int32, (CHUNK, 2 * CHUNK), 0)
    col = lax.broadcasted_iota(jnp.int32, (CHUNK, 2 * CHUNK), 1)
    lo = _lane_lo((CHUNK, LANES))
    qs = q * (HEAD_DIM ** -0.5)

    ya_blocks, yb_blocks = [], []
    for blk in range(n_blk):
        r0, r1 = blk * CHUNK, (blk + 1) * CHUNK
        ya_blocks.append(_spatial(va[r0:r1], u[r0:r1], ws_ref, bias))

        k_blk, v_blk = k[r0:r1], v[r0:r1]
        k_new = (k_blk.astype(_bf16), _swap_halves(k_blk).astype(_bf16))
        v_new = (v_blk.astype(_bf16), _swap_halves(v_blk).astype(_bf16))
        if blk == 0:
            k_old = (kprev[0], kprev[1])
            v_old = (vprev[0], vprev[1])
            first = jnp.where(i == 0, 2 * CHUNK, 0)
        else:
            first = 0
        valid = jnp.where(col < CHUNK, col - row - first, row + CHUNK - col) >= 0
        k_ctx = [jnp.concatenate([k_old[a], k_new[a]], axis=0) for a in range(2)]
        v_ctx = [jnp.concatenate([v_old[a], v_new[a]], axis=0) for a in range(2)]
        k_old, v_old = k_new, v_new

        pairs = []
        for p in range(N_HEADS // 2):
            q_pair = qs[r0:r1, p * LANES:(p + 1) * LANES]
            halves = []
            for half in range(2):
                h = 2 * p + half
                kv = h // GQA_GROUP
                src = (kv + half) % 2
                qm = jnp.where(lo if half == 0 else ~lo, q_pair, 0.0).astype(_bf16)
                sc = jnp.where(valid, _dot_nt(qm, k_ctx[src]), NEG_BIG)
                sink = sinks_ref[h]
                m = jnp.maximum(jnp.max(sc, axis=-1, keepdims=True), sink)
                e = jnp.exp(sc - m)
                denom = jnp.sum(e, axis=-1, keepdims=True) + jnp.exp(sink - m)
                o = _dot(e.astype(_bf16), v_ctx[src])
                halves.append(o * (1.0 / denom))
            pairs.append(jnp.where(lo, halves[0], halves[1]))
        yb_blocks.append(jnp.concatenate(pairs, axis=1))

    kprev[0], kprev[1] = k_old
    vprev[0], vprev[1] = v_old
    ya = jnp.concatenate(ya_blocks, axis=0)
    yb = jnp.concatenate(yb_blocks, axis=0)
    x_new, hn = _merge(x, ya, yb, g_oa, g_ob, w_out, g_ffn)
    y_ref[...] = _ffn(x_new, hn, w_up, w_down)


def _sample_kernel(x_ref, rope_ref, ck_ref, cv_ref, sink_rows, g_mix, w_in, g_sv, g_q, g_k, ones_bd,
                   ws_ref, bias_ref, g_oa, g_ob, w_out, g_ffn, w_up, w_down,
                   y_ref, ko_ref, vo_ref, va_ref, x_all):
    l = pl.program_id(0)
    i = pl.program_id(1)
    rows = x_ref.shape[0]
    nb = ck_ref.shape[0]
    dec_seq = rows // nb
    grp_rows = GQA_GROUP * dec_seq

    @pl.when(l == 0)
    def _():
        x_all[i] = x_ref[...]

    x = x_all[i]
    u, va, q, k, v = _split_inputs(_in_proj(x, g_mix, w_in), rope_ref, g_sv, g_q, g_k, ones_bd)
    va_ref[...] = va

    bias = bias_ref[...]
    ya = jnp.concatenate(
        [_spatial(va[r:r + CHUNK], u[r:r + CHUNK], ws_ref, bias) for r in range(0, rows, CHUNK)], axis=0)

    ck, cv = ck_ref[...], cv_ref[...]
    k_t, v_t = k.T, v.T
    lane = lax.broadcasted_iota(jnp.int32, (KV_COLS, WINDOW), 1)
    for b in range(nb):
        col = b * dec_seq
        grp, off = col // LANES, col % LANES
        for new_t, cache, out_ref in ((k_t, ck, ko_ref), (v_t, cv, vo_ref)):
            fresh = pltpu.roll(new_t[:, grp * LANES:(grp + 1) * LANES], (LANES - off) % LANES, 1)
            merged = jnp.where(lane < dec_seq, fresh, cache[b])
            out_ref[b] = pltpu.roll(merged, WINDOW - dec_seq, 1)

    ck16, cv16 = ck.astype(_bf16), cv.astype(_bf16)
    kt16 = k_t.astype(_bf16)
    q3 = (q * (HEAD_DIM ** -0.5)).reshape(nb, dec_seq, B_WIDTH)
    t_c = lax.broadcasted_iota(jnp.int32, (grp_rows, WINDOW), 0) % dec_seq
    w_c = lax.broadcasted_iota(jnp.int32, (grp_rows, WINDOW), 1)
    valid_c = (w_c >= t_c)[None]
    b_n = lax.broadcasted_iota(jnp.int32, (nb, grp_rows, rows), 0)
    t_n = lax.broadcasted_iota(jnp.int32, (nb, grp_rows, rows), 1) % dec_seq
    c_n = lax.broadcasted_iota(jnp.int32, (nb, grp_rows, rows), 2)
    t_key = c_n - b_n * dec_seq
    valid_n = jnp.minimum(t_key, t_n - t_key) >= 0

    heads = []
    for kv in range(KV_HEADS):
        d0, d1 = kv * HEAD_DIM, (kv + 1) * HEAD_DIM
        q_st = jnp.concatenate(
            [q3[:, :, h * HEAD_DIM:(h + 1) * HEAD_DIM] for h in range(kv * GQA_GROUP, (kv + 1) * GQA_GROUP)],
            axis=1).astype(_bf16)
        s_c = jnp.einsum('bqd,bdw->bqw', q_st, ck16[:, d0:d1, :], preferred_element_type=_f32)
        s_n = _dot(q_st.reshape(nb * grp_rows, HEAD_DIM), kt16[d0:d1, :]).reshape(nb, grp_rows, rows)
        s_c = jnp.where(valid_c, s_c, NEG_BIG)
        s_n = jnp.where(valid_n, s_n, NEG_BIG)
        sink = sink_rows[kv][None]
        m = jnp.maximum(jnp.maximum(jnp.max(s_c, axis=-1, keepdims=True),
                                    jnp.max(s_n, axis=-1, keepdims=True)), sink)
        e_c, e_n = jnp.exp(s_c - m), jnp.exp(s_n - m)
        denom = (jnp.sum(e_c, axis=-1, keepdims=True) + jnp.sum(e_n, axis=-1, keepdims=True)
                 + jnp.exp(sink - m))
        o = jnp.einsum('bqw,bdw->bqd', e_c.astype(_bf16), cv16[:, d0:d1, :], preferred_element_type=_f32)
        o = o + _dot(e_n.reshape(nb * grp_rows, rows).astype(_bf16),
                     v[:, d0:d1].astype(_bf16)).reshape(nb, grp_rows, HEAD_DIM)
        o = o * (1.0 / denom)
        heads += [o[:, g * dec_seq:(g + 1) * dec_seq, :] for g in range(GQA_GROUP)]
    yb = jnp.concatenate(heads, axis=2).reshape(rows, B_WIDTH)
    x_new, hn = _merge(x, ya, yb, g_oa, g_ob, w_out, g_ffn)
    y = _ffn(x_new, hn, w_up, w_down)
    x_all[i] = y
    y_ref[...] = y


def _rope_tables(pos):
    half = ROT_DIM // 2
    inv = np.power(np.float32(ROPE_THETA), -2.0 * np.arange(half, dtype=np.float32) / np.float32(ROT_DIM))
    ang = pos.astype(np.float32)[:, None] * inv.astype(np.float32)[None, :]
    cos, sin = np.cos(ang).astype(np.float32), np.sin(ang).astype(np.float32)
    n = pos.shape[0]
    pad1 = np.ones((n, HEAD_DIM - ROT_DIM), np.float32)
    pad0 = np.zeros((n, HEAD_DIM - ROT_DIM), np.float32)
    zero = np.zeros((n, half), np.float32)
    cos_t = np.concatenate([cos, cos, pad1], axis=1)
    sin_a = np.concatenate([-sin, zero, pad0], axis=1)
    sin_b = np.concatenate([zero, sin, pad0], axis=1)
    rep = LANES // HEAD_DIM
    return np.concatenate([np.tile(t, (1, rep)) for t in (cos_t, sin_a, sin_b)], axis=1)


def kernel(x_prompt, x_sample, cache_win_k, cache_win_v, g_mix, w_in, g_sv, g_q, g_k, w_spatial,
           b_spatial, sinks, g_out_a, g_out_b, w_out, g_ffn, w_up, w_down):
    batch, seq, _ = x_prompt.shape
    dec_batch, dec_seq, _ = x_sample.shape
    depth = w_in.shape[0]
    assert seq % PROMPT_TILE == 0 and PROMPT_TILE % CHUNK == 0
    assert dec_batch % SAMPLE_BATCH_TILE == 0 and CHUNK % dec_seq == 0
    s_rows = SAMPLE_BATCH_TILE * dec_seq
    assert s_rows % CHUNK == 0
    n_tiles = seq // PROMPT_TILE
    s_tiles = dec_batch // SAMPLE_BATCH_TILE

    rope_p = jnp.asarray(_rope_tables(np.arange(seq)))
    rope_s = jnp.asarray(np.tile(_rope_tables(seq + np.arange(dec_seq)), (SAMPLE_BATCH_TILE, 1)))
    hid = np.arange(B_WIDTH) // HEAD_DIM
    ones_bd = jnp.asarray(hid[:, None] == hid[None, :], dtype=_bf16)
    tril = np.tril(np.ones((CHUNK, CHUNK), dtype=bool))
    rep = CHUNK // dec_seq
    expand = jnp.asarray(np.tile(np.eye(dec_seq, dtype=np.float32), (rep, 1)))
    same_batch = jnp.asarray(np.kron(np.eye(rep), np.ones((dec_seq, dec_seq))), dtype=_f32)

    row3 = lambda a: a[:, None, :]
    ws = jnp.where(tril, w_spatial, 0.0)
    bias = jnp.repeat(jnp.swapaxes(b_spatial, 1, 2), HEAD_DIM, axis=2)
    ws_s = jnp.einsum('rt,lhts,cs->lhrc', expand, ws[:, :, :dec_seq, :dec_seq], expand,
                      precision=lax.Precision.HIGHEST) * same_batch
    bias_s = jnp.tile(bias[:, :dec_seq], (1, rep, 1))
    sink_rows = jnp.repeat(sinks.reshape(depth, KV_HEADS, GQA_GROUP), dec_seq, axis=2)[..., None]
    head = (row3(g_mix), w_in.astype(_bf16), row3(g_sv), row3(jnp.tile(g_q, (1, N_HEADS))),
            row3(jnp.tile(g_k, (1, KV_HEADS))))
    tail = (row3(g_out_a), row3(g_out_b), w_out.astype(_bf16), row3(g_ffn),
            w_up.astype(_bf16), w_down.astype(_bf16))
    head_shapes = [(1, D_MODEL), (D_MODEL, IN_COLS), (1, A_WIDTH), (1, B_WIDTH), (1, KV_COLS)]
    tail_shapes = [(1, A_WIDTH), (1, B_WIDTH), (D_MODEL, D_MODEL), (1, D_MODEL), (D_MODEL, D_FF), (D_FF, D_MODEL)]
    spatial_shapes = [(A_HEADS, CHUNK, CHUNK), (CHUNK, A_WIDTH)]

    def const_spec(shape):
        nd = len(shape)
        return pl.BlockSpec(shape, lambda *_: (0,) * nd, pipeline_mode=pl.Buffered(1))

    def layer_spec(shape, layer_of):
        nd = len(shape)
        return pl.BlockSpec((None,) + tuple(shape), lambda *g: (layer_of(*g),) + (0,) * nd,
                            pipeline_mode=pl.Buffered(1))

    def weight_specs(layer_of):
        return ([layer_spec(sh, layer_of) for sh in head_shapes] + [const_spec((B_WIDTH, B_WIDTH))]
                + [layer_spec(sh, layer_of) for sh in spatial_shapes + tail_shapes])

    cparams = functools.partial(pltpu.CompilerParams, vmem_limit_bytes=VMEM_LIMIT_BYTES)

    def prompt_layer(l, xp):
        return pl.pallas_call(
            _prompt_kernel,
            grid=(batch, n_tiles),
            in_specs=[pl.BlockSpec(memory_space=pltpu.SMEM),
                      pl.BlockSpec((None, PROMPT_TILE, D_MODEL), lambda b, i: (b, i, 0)),
                      pl.BlockSpec((PROMPT_TILE, 3 * LANES), lambda b, i: (i, 0))]
                     + weight_specs(lambda b, i: l),
            out_specs=[pl.BlockSpec((None, PROMPT_TILE, D_MODEL), lambda b, i: (b, i, 0)),
                       pl.BlockSpec((None, WINDOW, KV_COLS), lambda b, i: (b, 0, 0)),
                       pl.BlockSpec((None, WINDOW, KV_COLS), lambda b, i: (b, 0, 0))],
            out_shape=[jax.ShapeDtypeStruct((batch, seq, D_MODEL), _f32),
                       jax.ShapeDtypeStruct((batch, WINDOW, KV_COLS), _f32),
                       jax.ShapeDtypeStruct((batch, WINDOW, KV_COLS), _f32)],
            scratch_shapes=[pltpu.VMEM((2, CHUNK, KV_COLS), _bf16), pltpu.VMEM((2, CHUNK, KV_COLS), _bf16)],
            compiler_params=cparams(dimension_semantics=("arbitrary", "arbitrary")),
            name="prompt_layer",
        )(sinks[l], xp, rope_p, *head, ones_bd, ws.astype(_bf16), bias, *tail)

    xp = x_prompt
    nk_p, nv_p = [], []
    for l in range(depth):
        xp, kp, vp = prompt_layer(l, xp)
        nk_p.append(kp)
        nv_p.append(vp)

    win_t = lambda c: jnp.transpose(c, (0, 1, 3, 4, 2)).reshape(depth, dec_batch, KV_COLS, WINDOW)
    win = lambda c: jnp.transpose(c.reshape(depth, dec_batch, KV_HEADS, HEAD_DIM, WINDOW), (0, 1, 4, 2, 3))
    layer_tile = lambda l, i: (l, i, 0, 0)
    ys, ks, vs, va_s = pl.pallas_call(
        _sample_kernel,
        grid=(depth, s_tiles),
        in_specs=[pl.BlockSpec((s_rows, D_MODEL), lambda l, i: (jnp.where(l == 0, i, s_tiles - 1), 0)),
                  const_spec((s_rows, 3 * LANES)),
                  pl.BlockSpec((None, SAMPLE_BATCH_TILE, KV_COLS, WINDOW), layer_tile),
                  pl.BlockSpec((None, SAMPLE_BATCH_TILE, KV_COLS, WINDOW), layer_tile),
                  layer_spec((KV_HEADS, GQA_GROUP * dec_seq, 1), lambda l, i: l)]
                 + weight_specs(lambda l, i: l),
        out_specs=[pl.BlockSpec((s_rows, D_MODEL), lambda l, i: (jnp.where(l == depth - 1, i, 0), 0)),
                   pl.BlockSpec((None, SAMPLE_BATCH_TILE, KV_COLS, WINDOW), layer_tile),
                   pl.BlockSpec((None, SAMPLE_BATCH_TILE, KV_COLS, WINDOW), layer_tile),
                   pl.BlockSpec((None, s_rows, A_WIDTH), lambda l, i: (l, i, 0))],
        out_shape=[jax.ShapeDtypeStruct((dec_batch * dec_seq, D_MODEL), _f32),
                   jax.ShapeDtypeStruct((depth, dec_batch, KV_COLS, WINDOW), _f32),
                   jax.ShapeDtypeStruct((depth, dec_batch, KV_COLS, WINDOW), _f32),
                   jax.ShapeDtypeStruct((depth, dec_batch * dec_seq, A_WIDTH), _f32)],
        scratch_shapes=[pltpu.VMEM((s_tiles, s_rows, D_MODEL), _f32)],
        compiler_params=cparams(dimension_semantics=("arbitrary", "arbitrary")),
        name="sample_layers",
    )(x_sample.reshape(dec_batch * dec_seq, D_MODEL), rope_s,
      win_t(cache_win_k), win_t(cache_win_v), sink_rows,
      *head, ones_bd, ws_s.astype(_bf16), bias_s, *tail)

    kv5 = lambda t, n: t.reshape(depth, n, WINDOW, KV_HEADS, HEAD_DIM)
    return (xp, ys.reshape(dec_batch, dec_seq, D_MODEL),
            kv5(jnp.stack(nk_p), batch), kv5(jnp.stack(nv_p), batch), win(ks), win(vs),
            va_s.reshape(depth, dec_batch, dec_seq, A_HEADS, HEAD_DIM))
```

```python
import functools

import jax
import jax.numpy as jnp
import numpy as np
from jax import lax
from jax.experimental import pallas as pl
from jax.experimental.pallas import tpu as pltpu

D_MODEL = 1024
HEAD_DIM = 64
A_WIDTH = 512
B_WIDTH = 512
A_HEADS = A_WIDTH // HEAD_DIM
N_HEADS = B_WIDTH // HEAD_DIM
KV_HEADS = 2
GQA_GROUP = N_HEADS // KV_HEADS
KV_COLS = KV_HEADS * HEAD_DIM
CHUNK = 128
WINDOW = 128
ROPE_THETA = 500000.0
ROT_DIM = HEAD_DIM // 4
D_FF = 4 * D_MODEL
EPS = 1e-6
IN_COLS = 2 * A_WIDTH + B_WIDTH + 2 * KV_COLS
Q_OFF = 2 * A_WIDTH
K_OFF = Q_OFF + B_WIDTH
V_OFF = K_OFF + KV_COLS
NEG_BIG = -1e30

LANES = 128
FF_CHUNK = 1024
PROMPT_TILE = 512
SAMPLE_BATCH_TILE = 16
VMEM_LIMIT_BYTES = 56 * 1024 * 1024

_bf16 = jnp.bfloat16
_f32 = jnp.float32


def _dot(a, b):
    return jnp.dot(a, b, preferred_element_type=_f32)


def _dot_nt(a, b):
    return lax.dot_general(a, b, (((1,), (1,)), ((), ())), preferred_element_type=_f32)


def _rms(x, g):
    return x * lax.rsqrt(jnp.mean(x * x, axis=-1, keepdims=True) + EPS) * g


def _gelu(x):
    return 0.5 * x * (1.0 + lax.erf(x * np.float32(np.sqrt(0.5))))


def _head_rms(z, g_tiled, ones_bd):
    ss = _dot((z * z).astype(_bf16), ones_bd)
    return z * lax.rsqrt(ss * (1.0 / HEAD_DIM) + EPS) * g_tiled


def _rope(x, cos_t, sin_a, sin_b):
    cols = []
    for c in range(x.shape[1] // LANES):
        xc = x[:, c * LANES:(c + 1) * LANES]
        nxt = pltpu.roll(xc, LANES - ROT_DIM // 2, 1)
        prv = pltpu.roll(xc, ROT_DIM // 2, 1)
        cols.append(xc * cos_t + nxt * sin_a + prv * sin_b)
    return cols[0] if len(cols) == 1 else jnp.concatenate(cols, axis=1)


def _in_proj(x, g_mix, w_in):
    return _dot(_rms(x, g_mix[...]).astype(_bf16), w_in[...])


def _split_inputs(z, rope_ref, g_sv, g_q, g_k, ones_bd):
    u = _gelu(z[:, :A_WIDTH])
    va = _rms(_gelu(z[:, A_WIDTH:2 * A_WIDTH]), g_sv[...])
    cos_t = rope_ref[:, 0:LANES]
    sin_a = rope_ref[:, LANES:2 * LANES]
    sin_b = rope_ref[:, 2 * LANES:3 * LANES]
    ones = ones_bd[...]
    q = _rope(_head_rms(z[:, Q_OFF:K_OFF], g_q[...], ones), cos_t, sin_a, sin_b)
    k = _rope(_head_rms(z[:, K_OFF:V_OFF], g_k[...], ones[:KV_COLS, :KV_COLS]), cos_t, sin_a, sin_b)
    v = z[:, V_OFF:]
    return u, va, q, k, v


def _spatial(va_blk, u_blk, ws_ref, bias):
    vb = va_blk.astype(_bf16)
    outs = [_dot(ws_ref[h], vb[:, h * HEAD_DIM:(h + 1) * HEAD_DIM]) for h in range(A_HEADS)]
    return u_blk * (jnp.concatenate(outs, axis=1) + bias)


def _merge(x, ya, yb, g_oa, g_ob, w_out, g_ffn):
    cat = jnp.concatenate([_rms(ya, g_oa[...]), _rms(yb, g_ob[...])], axis=1).astype(_bf16)
    x = x + _dot(cat, w_out[...])
    return x, _rms(x, g_ffn[...]).astype(_bf16)


def _ffn_chunk(acc, hn, w_up, w_down, c, width):
    h = _dot(hn, w_up[:, c * width:(c + 1) * width])
    h = jnp.maximum(h, 0.0)
    return acc + _dot((h * h).astype(_bf16), w_down[c * width:(c + 1) * width, :])


def _ffn(x, hn, w_up, w_down):
    acc = x
    for c in range(D_FF // FF_CHUNK):
        acc = _ffn_chunk(acc, hn, w_up, w_down, c, FF_CHUNK)
    return acc


def _kv_lane_halves(k_blk, v_blk, lo):
    out = []
    for j in range(KV_HEADS):
        per = []
        for t in (k_blk, v_blk):
            sw = pltpu.roll(t, HEAD_DIM, 1)
            on_lo, on_hi = (t, sw) if j == 0 else (sw, t)
            per.append((jnp.where(lo, on_lo, 0.0), jnp.where(lo, 0.0, on_hi)))
        out.append(tuple(per))
    return out


def _lane_lo(shape):
    return lax.broadcasted_iota(jnp.int32, shape, len(shape) - 1) < HEAD_DIM


def _prompt_kernel(sinks_ref, x_ref, rope_ref, g_mix, w_in, g_sv, g_q, g_k, ones_bd, ws_ref,
                   bias_ref, g_oa, g_ob, w_out, g_ffn, w_up, w_down, *rest, n_cast):
    cast_src, rest = rest[:n_cast], rest[n_cast:]
    y_ref, ks_ref, vs_ref = rest[:3]
    cast_dst, (kprev, vprev) = rest[3:3 + n_cast], rest[3 + n_cast:]
    for src, dst in zip(cast_src, cast_dst):
        dst[...] = src[...].astype(_bf16)

    i = pl.program_id(1)
    tile = x_ref.shape[0]
    n_blk = tile // CHUNK

    @pl.when(i == 0)
    def _():
        kprev[...] = jnp.zeros_like(kprev)
        vprev[...] = jnp.zeros_like(vprev)

    x = x_ref[...]
    u, va, q, k, v = _split_inputs(_in_proj(x, g_mix, w_in), rope_ref, g_sv, g_q, g_k, ones_bd)
    ks_ref[...] = k[tile - WINDOW:, :]
    vs_ref[...] = v[tile - WINDOW:, :]

    bias = bias_ref[...]
    row = lax.broadcasted_iota(jnp.int32, (CHUNK, 2 * CHUNK), 0)
    col = lax.broadcasted_iota(jnp.int32, (CHUNK, 2 * CHUNK), 1)
    lo = _lane_lo((CHUNK, LANES))
    qs = q * (HEAD_DIM ** -0.5)

    ya_blocks, yb_blocks = [], []
    for blk in range(n_blk):
        r0, r1 = blk * CHUNK, (blk + 1) * CHUNK
        ya_blocks.append(_spatial(va[r0:r1], u[r0:r1], ws_ref, bias))

        if blk == 0:
            kv_old = [(kprev[2 * j], kprev[2 * j + 1], vprev[2 * j], vprev[2 * j + 1]) for j in range(KV_HEADS)]
            first = jnp.where(i == 0, 2 * CHUNK, 0)
        else:
            first = 0
        kv_new = []
        for j, (t_k, t_v) in enumerate(_kv_lane_halves(k[r0:r1], v[r0:r1], lo)):
            kv_new.append(tuple(a.astype(_bf16) for a in t_k + t_v))
        valid = jnp.where(col < CHUNK, col - row - first, row + CHUNK - col) >= 0
        k_ctx = [jnp.concatenate([kv_old[j][0], kv_new[j][0], kv_old[j][1], kv_new[j][1]], axis=0)
                 for j in range(KV_HEADS)]
        v_ctx = [jnp.concatenate([kv_old[j][2], kv_new[j][2], kv_old[j][3], kv_new[j][3]], axis=0)
                 for j in range(KV_HEADS)]
        kv_old = kv_new

        def scores(p):
            q_pair = qs[r0:r1, p * LANES:(p + 1) * LANES].astype(_bf16)
            return _dot_nt(q_pair, k_ctx[(2 * p) // GQA_GROUP])

        pairs = []
        n_pairs = N_HEADS // 2
        sc_all = [scores(p) for p in range(n_pairs)]
        for p in range(n_pairs):
            kv = (2 * p) // GQA_GROUP
            sc = sc_all[p]
            e_halves, r_den = [], []
            for half in range(2):
                s_h = jnp.where(valid, sc[:, half * 2 * CHUNK:(half + 1) * 2 * CHUNK], NEG_BIG)
                sink = sinks_ref[2 * p + half]
                m = jnp.maximum(jnp.max(s_h, axis=-1, keepdims=True), sink)
                e = jnp.exp(s_h - m)
                r_den.append(1.0 / (jnp.sum(e, axis=-1, keepdims=True) + jnp.exp(sink - m)))
                e_halves.append(e.astype(_bf16))
            o = _dot(jnp.concatenate(e_halves, axis=1), v_ctx[kv])
            pairs.append(o * jnp.where(lo, r_den[0], r_den[1]))
        yb_blocks.append(jnp.concatenate(pairs, axis=1))

    for j in range(KV_HEADS):
        kprev[2 * j], kprev[2 * j + 1], vprev[2 * j], vprev[2 * j + 1] = kv_old[j]
    ya = jnp.concatenate(ya_blocks, axis=0)
    yb = jnp.concatenate(yb_blocks, axis=0)
    x_new, hn = _merge(x, ya, yb, g_oa, g_ob, w_out, g_ffn)
    y_ref[...] = _ffn(x_new, hn, w_up, w_down)


def _sample_kernel(x_ref, rope_ref, ck_ref, cv_ref, sink_rows, g_mix, w_in, g_sv, g_q, g_k, ones_bd,
                   ws_ref, bias_ref, g_oa, g_ob, w_out, g_ffn, w_up, w_down,
                   y_ref, ko_ref, vo_ref, va_ref, x_all):
    l = pl.program_id(0)
    i = pl.program_id(1)
    rows = x_ref.shape[0]
    nb = ck_ref.shape[0]
    dec_seq = rows // nb
    grp_rows = GQA_GROUP * dec_seq

    @pl.when(l == 0)
    def _():
        x_all[i] = x_ref[...]

    x = x_all[i]
    u, va, q, k, v = _split_inputs(_in_proj(x, g_mix, w_in), rope_ref, g_sv, g_q, g_k, ones_bd)
    va_ref[...] = va

    bias = bias_ref[...]
    ya = jnp.concatenate(
        [_spatial(va[r:r + CHUNK], u[r:r + CHUNK], ws_ref, bias) for r in range(0, rows, CHUNK)], axis=0)

    ck, cv = ck_ref[...], cv_ref[...]
    k_t, v_t = k.T, v.T
    lane = lax.broadcasted_iota(jnp.int32, (KV_COLS, WINDOW), 1)
    for b in range(nb):
        col = b * dec_seq
        grp, off = col // LANES, col % LANES
        for new_t, cache, out_ref in ((k_t, ck, ko_ref), (v_t, cv, vo_ref)):
            fresh = pltpu.roll(new_t[:, grp * LANES:(grp + 1) * LANES], (LANES - off) % LANES, 1)
            merged = jnp.where(lane < dec_seq, fresh, cache[b])
            out_ref[b] = pltpu.roll(merged, WINDOW - dec_seq, 1)

    ck16, cv16 = ck.astype(_bf16), cv.astype(_bf16)
    kt16 = k_t.astype(_bf16)
    q3 = (q * (HEAD_DIM ** -0.5)).reshape(nb, dec_seq, B_WIDTH)
    t_c = lax.broadcasted_iota(jnp.int32, (grp_rows, WINDOW), 0) % dec_seq
    w_c = lax.broadcasted_iota(jnp.int32, (grp_rows, WINDOW), 1)
    valid_c = (w_c >= t_c)[None]
    b_n = lax.broadcasted_iota(jnp.int32, (nb, grp_rows, rows), 0)
    t_n = lax.broadcasted_iota(jnp.int32, (nb, grp_rows, rows), 1) % dec_seq
    c_n = lax.broadcasted_iota(jnp.int32, (nb, grp_rows, rows), 2)
    t_key = c_n - b_n * dec_seq
    valid_n = jnp.minimum(t_key, t_n - t_key) >= 0

    heads = []
    for kv in range(KV_HEADS):
        d0, d1 = kv * HEAD_DIM, (kv + 1) * HEAD_DIM
        q_st = jnp.concatenate(
            [q3[:, :, h * HEAD_DIM:(h + 1) * HEAD_DIM] for h in range(kv * GQA_GROUP, (kv + 1) * GQA_GROUP)],
            axis=1).astype(_bf16)
        s_c = jnp.einsum('bqd,bdw->bqw', q_st, ck16[:, d0:d1, :], preferred_element_type=_f32)
        s_n = _dot(q_st.reshape(nb * grp_rows, HEAD_DIM), kt16[d0:d1, :]).reshape(nb, grp_rows, rows)
        s_c = jnp.where(valid_c, s_c, NEG_BIG)
        s_n = jnp.where(valid_n, s_n, NEG_BIG)
        sink = sink_rows[kv][None]
        m = jnp.maximum(jnp.maximum(jnp.max(s_c, axis=-1, keepdims=True),
                                    jnp.max(s_n, axis=-1, keepdims=True)), sink)
        e_c, e_n = jnp.exp(s_c - m), jnp.exp(s_n - m)
        denom = (jnp.sum(e_c, axis=-1, keepdims=True) + jnp.sum(e_n, axis=-1, keepdims=True)
                 + jnp.exp(sink - m))
        o = jnp.einsum('bqw,bdw->bqd', e_c.astype(_bf16), cv16[:, d0:d1, :], preferred_element_type=_f32)
        o = o + _dot(e_n.reshape(nb * grp_rows, rows).astype(_bf16),
                     v[:, d0:d1].astype(_bf16)).reshape(nb, grp_rows, HEAD_DIM)
        o = o * (1.0 / denom)
        heads += [o[:, g * dec_seq:(g + 1) * dec_seq, :] for g in range(GQA_GROUP)]
    yb = jnp.concatenate(heads, axis=2).reshape(rows, B_WIDTH)
    x_new, hn = _merge(x, ya, yb, g_oa, g_ob, w_out, g_ffn)
    y = _ffn(x_new, hn, w_up, w_down)
    x_all[i] = y
    y_ref[...] = y


def _rope_tables(pos):
    half = ROT_DIM // 2
    inv = np.power(np.float32(ROPE_THETA), -2.0 * np.arange(half, dtype=np.float32) / np.float32(ROT_DIM))
    ang = pos.astype(np.float32)[:, None] * inv.astype(np.float32)[None, :]
    cos, sin = np.cos(ang).astype(np.float32), np.sin(ang).astype(np.float32)
    n = pos.shape[0]
    pad1 = np.ones((n, HEAD_DIM - ROT_DIM), np.float32)
    pad0 = np.zeros((n, HEAD_DIM - ROT_DIM), np.float32)
    zero = np.zeros((n, half), np.float32)
    cos_t = np.concatenate([cos, cos, pad1], axis=1)
    sin_a = np.concatenate([-sin, zero, pad0], axis=1)
    sin_b = np.concatenate([zero, sin, pad0], axis=1)
    rep = LANES // HEAD_DIM
    return np.concatenate([np.tile(t, (1, rep)) for t in (cos_t, sin_a, sin_b)], axis=1)


def kernel(x_prompt, x_sample, cache_win_k, cache_win_v, g_mix, w_in, g_sv, g_q, g_k, w_spatial,
           b_spatial, sinks, g_out_a, g_out_b, w_out, g_ffn, w_up, w_down):
    batch, seq, _ = x_prompt.shape
    dec_batch, dec_seq, _ = x_sample.shape
    depth = w_in.shape[0]
    assert seq % PROMPT_TILE == 0 and PROMPT_TILE % CHUNK == 0
    assert dec_batch % SAMPLE_BATCH_TILE == 0 and CHUNK % dec_seq == 0
    s_rows = SAMPLE_BATCH_TILE * dec_seq
    assert s_rows % CHUNK == 0
    n_tiles = seq // PROMPT_TILE
    s_tiles = dec_batch // SAMPLE_BATCH_TILE

    rope_p = jnp.asarray(_rope_tables(np.arange(seq)))
    rope_s = jnp.asarray(np.tile(_rope_tables(seq + np.arange(dec_seq)), (SAMPLE_BATCH_TILE, 1)))
    hid = np.arange(B_WIDTH) // HEAD_DIM
    ones_bd = jnp.asarray(hid[:, None] == hid[None, :], dtype=_bf16)
    tril = np.tril(np.ones((CHUNK, CHUNK), dtype=bool))
    rep = CHUNK // dec_seq
    expand = jnp.asarray(np.tile(np.eye(dec_seq, dtype=np.float32), (rep, 1)))
    same_batch = jnp.asarray(np.kron(np.eye(rep), np.ones((dec_seq, dec_seq))), dtype=_f32)

    row3 = lambda a: a[:, None, :]
    ws = jnp.where(tril, w_spatial, 0.0)
    bias = jnp.repeat(jnp.swapaxes(b_spatial, 1, 2), HEAD_DIM, axis=2)
    ws_s = jnp.einsum('rt,lhts,cs->lhrc', expand, ws[:, :, :dec_seq, :dec_seq], expand,
                      precision=lax.Precision.HIGHEST) * same_batch
    bias_s = jnp.tile(bias[:, :dec_seq], (1, rep, 1))
    sink_rows = jnp.repeat(sinks.reshape(depth, KV_HEADS, GQA_GROUP), dec_seq, axis=2)[..., None]
    big_f32 = (w_in, w_out, w_up, w_down)
    head = lambda wb: (row3(g_mix), wb[0], row3(g_sv), row3(jnp.tile(g_q, (1, N_HEADS))),
                       row3(jnp.tile(g_k, (1, KV_HEADS))))
    tail = lambda wb: (row3(g_out_a), row3(g_out_b), wb[1], row3(g_ffn), wb[2], wb[3])
    head_shapes = [(1, D_MODEL), (D_MODEL, IN_COLS), (1, A_WIDTH), (1, B_WIDTH), (1, KV_COLS)]
    tail_shapes = [(1, A_WIDTH), (1, B_WIDTH), (D_MODEL, D_MODEL), (1, D_MODEL), (D_MODEL, D_FF), (D_FF, D_MODEL)]
    spatial_shapes = [(A_HEADS, CHUNK, CHUNK), (CHUNK, A_WIDTH)]

    def const_spec(shape):
        nd = len(shape)
        return pl.BlockSpec(shape, lambda *_: (0,) * nd, pipeline_mode=pl.Buffered(1))

    def layer_spec(shape, layer_of):
        nd = len(shape)
        return pl.BlockSpec((None,) + tuple(shape), lambda *g: (layer_of(*g),) + (0,) * nd,
                            pipeline_mode=pl.Buffered(1))

    def weight_specs(layer_of):
        return ([layer_spec(sh, layer_of) for sh in head_shapes] + [const_spec((B_WIDTH, B_WIDTH))]
                + [layer_spec(sh, layer_of) for sh in spatial_shapes + tail_shapes])

    cparams = functools.partial(pltpu.CompilerParams, vmem_limit_bytes=VMEM_LIMIT_BYTES)

    n_slabs = batch * n_tiles
    assert all(w.shape[1] % (n_slabs * 16) == 0 for w in big_f32)
    slab_shape = lambda w: (None, w.shape[1] // n_slabs, w.shape[2])

    def cast_first_layer_kernel(*refs):
        for src, dst in zip(refs[:len(big_f32)], refs[len(big_f32):]):
            dst[...] = src[...].astype(_bf16)

    wb = pl.pallas_call(
        cast_first_layer_kernel,
        grid=(n_slabs,),
        in_specs=[pl.BlockSpec(slab_shape(w), lambda s: (0, s, 0)) for w in big_f32],
        out_specs=[pl.BlockSpec(slab_shape(w), lambda s: (0, s, 0)) for w in big_f32],
        out_shape=[jax.ShapeDtypeStruct(w.shape, _bf16) for w in big_f32],
        compiler_params=cparams(dimension_semantics=("arbitrary",)),
        name="cast_first_layer",
    )(*big_f32)

    n_fixed_in = 3 + len(head_shapes) + 1 + len(spatial_shapes) + len(tail_shapes)
    big_in_pos = {0: 4, 1: 13, 2: 15, 3: 16}

    def prompt_layer(l, xp, wb):
        cast_next = l + 1 < depth
        slab_of = lambda b, i: (l + 1, b * n_tiles + i, 0)
        in_specs = ([pl.BlockSpec(memory_space=pltpu.SMEM),
                     pl.BlockSpec((None, PROMPT_TILE, D_MODEL), lambda b, i: (b, i, 0)),
                     pl.BlockSpec((PROMPT_TILE, 3 * LANES), lambda b, i: (i, 0))]
                    + weight_specs(lambda b, i: l))
        assert len(in_specs) == n_fixed_in
        out_specs = [pl.BlockSpec((None, PROMPT_TILE, D_MODEL), lambda b, i: (b, i, 0)),
                     pl.BlockSpec((None, WINDOW, KV_COLS), lambda b, i: (b, 0, 0)),
                     pl.BlockSpec((None, WINDOW, KV_COLS), lambda b, i: (b, 0, 0))]
        out_shape = [jax.ShapeDtypeStruct((batch, seq, D_MODEL), _f32),
                     jax.ShapeDtypeStruct((batch, WINDOW, KV_COLS), _f32),
                     jax.ShapeDtypeStruct((batch, WINDOW, KV_COLS), _f32)]
        args = (sinks[l], xp, rope_p, *head(wb), ones_bd, ws.astype(_bf16), bias, *tail(wb))
        aliases = {}
        if cast_next:
            in_specs += [pl.BlockSpec(slab_shape(w), slab_of) for w in big_f32]
            out_specs += [pl.BlockSpec(slab_shape(w), slab_of) for w in big_f32]
            out_shape += [jax.ShapeDtypeStruct(w.shape, _bf16) for w in big_f32]
            args += big_f32
            aliases = {big_in_pos[j]: 3 + j for j in range(len(big_f32))}
        outs = pl.pallas_call(
            functools.partial(_prompt_kernel, n_cast=len(big_f32) if cast_next else 0),
            grid=(batch, n_tiles),
            in_specs=in_specs, out_specs=out_specs, out_shape=out_shape,
            input_output_aliases=aliases,
            scratch_shapes=[pltpu.VMEM((2 * KV_HEADS, CHUNK, KV_COLS), _bf16),
                            pltpu.VMEM((2 * KV_HEADS, CHUNK, KV_COLS), _bf16)],
            compiler_params=cparams(dimension_semantics=("arbitrary", "arbitrary")),
            name="prompt_layer",
        )(*args)
        return outs[0], outs[1], outs[2], (tuple(outs[3:]) if cast_next else wb)

    xp = x_prompt
    nk_p, nv_p = [], []
    for l in range(depth):
        xp, kp, vp, wb = prompt_layer(l, xp, wb)
        nk_p.append(kp)
        nv_p.append(vp)

    win_t = lambda c: jnp.transpose(c, (0, 1, 3, 4, 2)).reshape(depth, dec_batch, KV_COLS, WINDOW)
    win = lambda c: jnp.transpose(c.reshape(depth, dec_batch, KV_HEADS, HEAD_DIM, WINDOW), (0, 1, 4, 2, 3))
    layer_tile = lambda l, i: (l, i, 0, 0)
    ys, ks, vs, va_s = pl.pallas_call(
        _sample_kernel,
        grid=(depth, s_tiles),
        in_specs=[pl.BlockSpec((s_rows, D_MODEL), lambda l, i: (jnp.where(l == 0, i, s_tiles - 1), 0)),
                  const_spec((s_rows, 3 * LANES)),
                  pl.BlockSpec((None, SAMPLE_BATCH_TILE, KV_COLS, WINDOW), layer_tile),
                  pl.BlockSpec((None, SAMPLE_BATCH_TILE, KV_COLS, WINDOW), layer_tile),
                  layer_spec((KV_HEADS, GQA_GROUP * dec_seq, 1), lambda l, i: l)]
                 + weight_specs(lambda l, i: l),
        out_specs=[pl.BlockSpec((s_rows, D_MODEL), lambda l, i: (jnp.where(l == depth - 1, i, 0), 0)),
                   pl.BlockSpec((None, SAMPLE_BATCH_TILE, KV_COLS, WINDOW), layer_tile),
                   pl.BlockSpec((None, SAMPLE_BATCH_TILE, KV_COLS, WINDOW), layer_tile),
                   pl.BlockSpec((None, s_rows, A_WIDTH), lambda l, i: (l, i, 0))],
        out_shape=[jax.ShapeDtypeStruct((dec_batch * dec_seq, D_MODEL), _f32),
                   jax.ShapeDtypeStruct((depth, dec_batch, KV_COLS, WINDOW), _f32),
                   jax.ShapeDtypeStruct((depth, dec_batch, KV_COLS, WINDOW), _f32),
                   jax.ShapeDtypeStruct((depth, dec_batch * dec_seq, A_WIDTH), _f32)],
        scratch_shapes=[pltpu.VMEM((s_tiles, s_rows, D_MODEL), _f32)],
        compiler_params=cparams(dimension_semantics=("arbitrary", "arbitrary")),
        name="sample_layers",
    )(x_sample.reshape(dec_batch * dec_seq, D_MODEL), rope_s,
      win_t(cache_win_k), win_t(cache_win_v), sink_rows,
      *head(wb), ones_bd, ws_s.astype(_bf16), bias_s, *tail(wb))

    kv5 = lambda t, n: t.reshape(depth, n, WINDOW, KV_HEADS, HEAD_DIM)
    return (xp, ys.reshape(dec_batch, dec_seq, D_MODEL),
            kv5(jnp.stack(nk_p), batch), kv5(jnp.stack(nv_p), batch), win(ks), win(vs),
            va_s.reshape(depth, dec_batch, dec_seq, A_HEADS, HEAD_DIM))
```

```python
import functools

import jax
import jax.numpy as jnp
import numpy as np
from jax import lax
from jax.experimental import pallas as pl
from jax.experimental.pallas import tpu as pltpu

D_MODEL = 1024
HEAD_DIM = 64
A_WIDTH = 512
B_WIDTH = 512
A_HEADS = A_WIDTH // HEAD_DIM
N_HEADS = B_WIDTH // HEAD_DIM
KV_HEADS = 2
GQA_GROUP = N_HEADS // KV_HEADS
KV_COLS = KV_HEADS * HEAD_DIM
CHUNK = 128
WINDOW = 128
ROPE_THETA = 500000.0
ROT_DIM = HEAD_DIM // 4
D_FF = 4 * D_MODEL
EPS = 1e-6
IN_COLS = 2 * A_WIDTH + B_WIDTH + 2 * KV_COLS
Q_OFF = 2 * A_WIDTH
K_OFF = Q_OFF + B_WIDTH
V_OFF = K_OFF + KV_COLS
NEG_BIG = -1e30
LOG2_E = float(np.log2(np.e))

LANES = 128
FF_CHUNK = 1024
PROMPT_TILE = 512
SAMPLE_BATCH_TILE = 16
SAMPLE_ATTN_GROUP = 4
FIRST_CAST_STEPS = 8
VMEM_LIMIT_BYTES = 56 * 1024 * 1024

_bf16 = jnp.bfloat16
_f32 = jnp.float32


def _dot(a, b):
    return jnp.dot(a, b, preferred_element_type=_f32)


def _rms(x, g):
    return x * lax.rsqrt(jnp.mean(x * x, axis=-1, keepdims=True) + EPS) * g


def _gelu(x):
    return 0.5 * x * (1.0 + lax.erf(x * np.float32(np.sqrt(0.5))))


def _head_rms(z, g_tiled, ones_bd):
    ss = _dot((z * z).astype(_bf16), ones_bd)
    return z * lax.rsqrt(ss * (1.0 / HEAD_DIM) + EPS) * g_tiled


def _rope(x, cos_t, sin_a, sin_b):
    cols = []
    for c in range(x.shape[1] // LANES):
        xc = x[:, c * LANES:(c + 1) * LANES]
        nxt = pltpu.roll(xc, LANES - ROT_DIM // 2, 1)
        prv = pltpu.roll(xc, ROT_DIM // 2, 1)
        cols.append(xc * cos_t + nxt * sin_a + prv * sin_b)
    return cols[0] if len(cols) == 1 else jnp.concatenate(cols, axis=1)


def _in_proj(x, g_mix, w_in):
    return _dot(_rms(x, g_mix[...]).astype(_bf16), w_in[...])


def _split_inputs(z, rope_ref, g_sv, g_q, g_k, ones_bd):
    u = _gelu(z[:, :A_WIDTH])
    va = _rms(_gelu(z[:, A_WIDTH:2 * A_WIDTH]), g_sv[...])
    cos_t = rope_ref[:, 0:LANES]
    sin_a = rope_ref[:, LANES:2 * LANES]
    sin_b = rope_ref[:, 2 * LANES:3 * LANES]
    ones = ones_bd[...]
    q = _rope(_head_rms(z[:, Q_OFF:K_OFF], g_q[...], ones), cos_t, sin_a, sin_b)
    k = _rope(_head_rms(z[:, K_OFF:V_OFF], g_k[...], ones[:KV_COLS, :KV_COLS]), cos_t, sin_a, sin_b)
    v = z[:, V_OFF:]
    return u, va, q, k, v


def _spatial(va_blk, u_blk, ws_ref, bias):
    vb = va_blk.astype(_bf16)
    outs = [_dot(ws_ref[h], vb[:, h * HEAD_DIM:(h + 1) * HEAD_DIM]) for h in range(A_HEADS)]
    return u_blk * (jnp.concatenate(outs, axis=1) + bias)


def _merge(x, ya, yb, g_oa, g_ob, w_out, g_ffn):
    cat = jnp.concatenate([_rms(ya, g_oa[...]), _rms(yb, g_ob[...])], axis=1).astype(_bf16)
    x = x + _dot(cat, w_out[...])
    return x, _rms(x, g_ffn[...]).astype(_bf16)


def _ffn_chunk(acc, hn, w_up, w_down, c, width):
    h = _dot(hn, w_up[:, c * width:(c + 1) * width])
    h = jnp.maximum(h, 0.0)
    return acc + _dot((h * h).astype(_bf16), w_down[c * width:(c + 1) * width, :])


def _ffn(x, hn, w_up, w_down):
    acc = x
    for c in range(D_FF // FF_CHUNK):
        acc = _ffn_chunk(acc, hn, w_up, w_down, c, FF_CHUNK)
    return acc


def _prompt_kernel(sinks_ref, x_ref, rope_ref, g_mix, w_in, g_sv, g_q, g_k, ones_bd, ws_ref,
                   bias_ref, g_oa, g_ob, w_out, g_ffn, w_up, w_down, *rest, n_cast):
    cast_src, rest = rest[:n_cast], rest[n_cast:]
    y_ref, ks_ref, vs_ref = rest[:3]
    cast_dst, (kprev, vprev) = rest[3:3 + n_cast], rest[3 + n_cast:]
    for src, dst in zip(cast_src, cast_dst):
        dst[...] = src[...].astype(_bf16)

    i = pl.program_id(1)
    tile = x_ref.shape[0]
    n_blk = tile // CHUNK

    @pl.when(i == 0)
    def _():
        kprev[...] = jnp.zeros_like(kprev)
        vprev[...] = jnp.zeros_like(vprev)

    x = x_ref[...]
    u, va, q, k, v = _split_inputs(_in_proj(x, g_mix, w_in), rope_ref, g_sv, g_q, g_k, ones_bd)
    ks_ref[...] = k[tile - WINDOW:, :]
    vs_ref[...] = v[tile - WINDOW:, :]

    bias = bias_ref[...]
    key = lax.broadcasted_iota(jnp.int32, (2 * CHUNK, CHUNK), 0)
    qry = lax.broadcasted_iota(jnp.int32, (2 * CHUNK, CHUNK), 1)
    qs = q * (HEAD_DIM ** -0.5 * LOG2_E)
    zeros_half = jnp.zeros((HEAD_DIM, GQA_GROUP * CHUNK), _bf16)

    ya_blocks, yb_blocks = [], []
    for blk in range(n_blk):
        r0, r1 = blk * CHUNK, (blk + 1) * CHUNK
        ya_blocks.append(_spatial(va[r0:r1], u[r0:r1], ws_ref, bias))

        k_new = k[r0:r1].astype(_bf16)
        vt_new = v[r0:r1].T.astype(_bf16)
        if blk == 0:
            k_old, vt_old = kprev[...], vprev[...]
            first = jnp.where(i == 0, 2 * CHUNK, 0)
        else:
            first = 0
        k_ctx = jnp.concatenate([k_old, k_new], axis=0)
        vt_ctx = jnp.concatenate([vt_old, vt_new], axis=1)
        k_old, vt_old = k_new, vt_new
        valid = jnp.where(key < CHUNK, key - qry - first, qry + CHUNK - key) >= 0

        q_t = qs[r0:r1].T.astype(_bf16)
        out_t = []
        for kv in range(KV_HEADS):
            group = range(kv * GQA_GROUP, (kv + 1) * GQA_GROUP)
            q_grp = jnp.concatenate([q_t[h * HEAD_DIM:(h + 1) * HEAD_DIM] for h in group], axis=1)
            rhs = jnp.concatenate([q_grp, zeros_half] if kv == 0 else [zeros_half, q_grp], axis=0)
            sc = _dot(k_ctx, rhs)
            es, r_den = [], []
            for n, h in enumerate(group):
                s_h = jnp.where(valid, sc[:, n * CHUNK:(n + 1) * CHUNK], NEG_BIG)
                sink = sinks_ref[h] * LOG2_E
                m = jnp.maximum(jnp.max(s_h, axis=0, keepdims=True), sink)
                e = jnp.exp2(s_h - m)
                r_den.append(1.0 / (jnp.sum(e, axis=0, keepdims=True) + jnp.exp2(sink - m)))
                es.append(e.astype(_bf16))
            o_t = _dot(vt_ctx[kv * HEAD_DIM:(kv + 1) * HEAD_DIM], jnp.concatenate(es, axis=1))
            out_t += [o_t[:, n * CHUNK:(n + 1) * CHUNK] * r_den[n] for n in range(GQA_GROUP)]
        yb_blocks.append(jnp.concatenate(out_t, axis=0).T)

    kprev[...] = k_old
    vprev[...] = vt_old
    ya = jnp.concatenate(ya_blocks, axis=0)
    yb = jnp.concatenate(yb_blocks, axis=0)
    x_new, hn = _merge(x, ya, yb, g_oa, g_ob, w_out, g_ffn)
    y_ref[...] = _ffn(x_new, hn, w_up, w_down)


def _sample_kernel(x_ref, rope_ref, ck_ref, cv_ref, sink_rows, g_mix, w_in, g_sv, g_q, g_k, ones_bd,
                   ws_ref, bias_ref, g_oa, g_ob, w_out, g_ffn, w_up, w_down,
                   y_ref, ko_ref, vo_ref, va_ref, x_all):
    l = pl.program_id(0)
    i = pl.program_id(1)
    rows = x_ref.shape[0]
    nb = ck_ref.shape[0]
    dec_seq = rows // nb
    grp_rows = GQA_GROUP * dec_seq

    @pl.when(l == 0)
    def _():
        x_all[i] = x_ref[...]

    x = x_all[i]
    u, va, q, k, v = _split_inputs(_in_proj(x, g_mix, w_in), rope_ref, g_sv, g_q, g_k, ones_bd)
    va_ref[...] = va

    bias = bias_ref[...]
    ya = jnp.concatenate(
        [_spatial(va[r:r + CHUNK], u[r:r + CHUNK], ws_ref, bias) for r in range(0, rows, CHUNK)], axis=0)

    ck, cv = ck_ref[...], cv_ref[...]
    k_t, v_t = k.T, v.T
    lane = lax.broadcasted_iota(jnp.int32, (KV_COLS, WINDOW), 1)
    for b in range(nb):
        col = b * dec_seq
        grp, off = col // LANES, col % LANES
        for new_t, cache, out_ref in ((k_t, ck, ko_ref), (v_t, cv, vo_ref)):
            fresh = pltpu.roll(new_t[:, grp * LANES:(grp + 1) * LANES], (LANES - off) % LANES, 1)
            merged = jnp.where(lane < dec_seq, fresh, cache[b])
            out_ref[b] = pltpu.roll(merged, WINDOW - dec_seq, 1)

    gb = SAMPLE_ATTN_GROUP
    ng = nb // gb
    g_rows = gb * grp_rows
    ck16, cv16 = ck.astype(_bf16), cv.astype(_bf16)
    kt16 = k_t.astype(_bf16)
    q3 = (q * (HEAD_DIM ** -0.5)).reshape(nb, dec_seq, B_WIDTH)
    r_c = lax.broadcasted_iota(jnp.int32, (g_rows, gb * WINDOW), 0)
    c_c = lax.broadcasted_iota(jnp.int32, (g_rows, gb * WINDOW), 1)
    w_own = c_c - (r_c - r_c % grp_rows) * (WINDOW // grp_rows)
    valid_c = (jnp.minimum(w_own - r_c % dec_seq, WINDOW - 1 - w_own) >= 0)[None]
    g_n = lax.broadcasted_iota(jnp.int32, (ng, g_rows, rows), 0)
    r_n = lax.broadcasted_iota(jnp.int32, (ng, g_rows, rows), 1)
    c_n = lax.broadcasted_iota(jnp.int32, (ng, g_rows, rows), 2)
    t_key = c_n * GQA_GROUP - (g_n * g_rows + r_n - r_n % grp_rows)
    valid_n = jnp.minimum(t_key, (r_n % dec_seq) * GQA_GROUP - t_key) >= 0

    def side_by_side(t16, d0, d1):
        return jnp.stack([jnp.concatenate([t16[g * gb + j, d0:d1, :] for j in range(gb)], axis=1)
                          for g in range(ng)])

    heads = []
    for kv in range(KV_HEADS):
        d0, d1 = kv * HEAD_DIM, (kv + 1) * HEAD_DIM
        q_st = jnp.concatenate(
            [q3[:, :, h * HEAD_DIM:(h + 1) * HEAD_DIM] for h in range(kv * GQA_GROUP, (kv + 1) * GQA_GROUP)],
            axis=1).astype(_bf16)
        q_g = q_st.reshape(ng, g_rows, HEAD_DIM)
        s_c = jnp.einsum('gqd,gdw->gqw', q_g, side_by_side(ck16, d0, d1), preferred_element_type=_f32)
        s_n = _dot(q_st.reshape(nb * grp_rows, HEAD_DIM), kt16[d0:d1, :]).reshape(ng, g_rows, rows)
        s_c = jnp.where(valid_c, s_c, NEG_BIG)
        s_n = jnp.where(valid_n, s_n, NEG_BIG)
        sink = jnp.concatenate([sink_rows[kv]] * gb, axis=0)[None]
        m = jnp.maximum(jnp.maximum(jnp.max(s_c, axis=-1, keepdims=True),
                                    jnp.max(s_n, axis=-1, keepdims=True)), sink)
        e_c, e_n = jnp.exp(s_c - m), jnp.exp(s_n - m)
        denom = (jnp.sum(e_c, axis=-1, keepdims=True) + jnp.sum(e_n, axis=-1, keepdims=True)
                 + jnp.exp(sink - m))
        o = jnp.einsum('gqw,gdw->gqd', e_c.astype(_bf16), side_by_side(cv16, d0, d1),
                       preferred_element_type=_f32)
        o = o + _dot(e_n.reshape(nb * grp_rows, rows).astype(_bf16),
                     v[:, d0:d1].astype(_bf16)).reshape(ng, g_rows, HEAD_DIM)
        o = (o * (1.0 / denom)).reshape(nb, grp_rows, HEAD_DIM)
        heads += [o[:, g * dec_seq:(g + 1) * dec_seq, :] for g in range(GQA_GROUP)]
    yb = jnp.concatenate(heads, axis=2).reshape(rows, B_WIDTH)
    x_new, hn = _merge(x, ya, yb, g_oa, g_ob, w_out, g_ffn)
    y = _ffn(x_new, hn, w_up, w_down)
    x_all[i] = y
    y_ref[...] = y


def _rope_tables(pos):
    half = ROT_DIM // 2
    inv = np.power(np.float32(ROPE_THETA), -2.0 * np.arange(half, dtype=np.float32) / np.float32(ROT_DIM))
    ang = pos.astype(np.float32)[:, None] * inv.astype(np.float32)[None, :]
    cos, sin = np.cos(ang).astype(np.float32), np.sin(ang).astype(np.float32)
    n = pos.shape[0]
    pad1 = np.ones((n, HEAD_DIM - ROT_DIM), np.float32)
    pad0 = np.zeros((n, HEAD_DIM - ROT_DIM), np.float32)
    zero = np.zeros((n, half), np.float32)
    cos_t = np.concatenate([cos, cos, pad1], axis=1)
    sin_a = np.concatenate([-sin, zero, pad0], axis=1)
    sin_b = np.concatenate([zero, sin, pad0], axis=1)
    rep = LANES // HEAD_DIM
    return np.concatenate([np.tile(t, (1, rep)) for t in (cos_t, sin_a, sin_b)], axis=1)


def kernel(x_prompt, x_sample, cache_win_k, cache_win_v, g_mix, w_in, g_sv, g_q, g_k, w_spatial,
           b_spatial, sinks, g_out_a, g_out_b, w_out, g_ffn, w_up, w_down):
    batch, seq, _ = x_prompt.shape
    dec_batch, dec_seq, _ = x_sample.shape
    depth = w_in.shape[0]
    assert seq % PROMPT_TILE == 0 and PROMPT_TILE % CHUNK == 0
    assert dec_batch % SAMPLE_BATCH_TILE == 0 and CHUNK % dec_seq == 0
    s_rows = SAMPLE_BATCH_TILE * dec_seq
    assert s_rows % CHUNK == 0 and SAMPLE_BATCH_TILE % SAMPLE_ATTN_GROUP == 0
    assert WINDOW % (GQA_GROUP * dec_seq) == 0
    n_tiles = seq // PROMPT_TILE
    s_tiles = dec_batch // SAMPLE_BATCH_TILE

    rope_p = jnp.asarray(_rope_tables(np.arange(seq)))
    rope_s = jnp.asarray(np.tile(_rope_tables(seq + np.arange(dec_seq)), (SAMPLE_BATCH_TILE, 1)))
    hid = np.arange(B_WIDTH) // HEAD_DIM
    ones_bd = jnp.asarray(hid[:, None] == hid[None, :], dtype=_bf16)
    tril = np.tril(np.ones((CHUNK, CHUNK), dtype=bool))
    rep = CHUNK // dec_seq
    expand = jnp.asarray(np.tile(np.eye(dec_seq, dtype=np.float32), (rep, 1)))
    same_batch = jnp.asarray(np.kron(np.eye(rep), np.ones((dec_seq, dec_seq))), dtype=_f32)

    row3 = lambda a: a[:, None, :]
    ws = jnp.where(tril, w_spatial, 0.0)
    bias = jnp.repeat(jnp.swapaxes(b_spatial, 1, 2), HEAD_DIM, axis=2)
    ws_s = jnp.einsum('rt,lhts,cs->lhrc', expand, ws[:, :, :dec_seq, :dec_seq], expand,
                      precision=lax.Precision.HIGHEST) * same_batch
    bias_s = jnp.tile(bias[:, :dec_seq], (1, rep, 1))
    sink_rows = jnp.repeat(sinks.reshape(depth, KV_HEADS, GQA_GROUP), dec_seq, axis=2)[..., None]
    big_f32 = (w_in, w_out, w_up, w_down)
    head = lambda wb: (row3(g_mix), wb[0], row3(g_sv), row3(jnp.tile(g_q, (1, N_HEADS))),
                       row3(jnp.tile(g_k, (1, KV_HEADS))))
    tail = lambda wb: (row3(g_out_a), row3(g_out_b), wb[1], row3(g_ffn), wb[2], wb[3])
    head_shapes = [(1, D_MODEL), (D_MODEL, IN_COLS), (1, A_WIDTH), (1, B_WIDTH), (1, KV_COLS)]
    tail_shapes = [(1, A_WIDTH), (1, B_WIDTH), (D_MODEL, D_MODEL), (1, D_MODEL), (D_MODEL, D_FF), (D_FF, D_MODEL)]
    spatial_shapes = [(A_HEADS, CHUNK, CHUNK), (CHUNK, A_WIDTH)]

    def const_spec(shape):
        nd = len(shape)
        return pl.BlockSpec(shape, lambda *_: (0,) * nd, pipeline_mode=pl.Buffered(1))

    def layer_spec(shape, layer_of):
        nd = len(shape)
        return pl.BlockSpec((None,) + tuple(shape), lambda *g: (layer_of(*g),) + (0,) * nd,
                            pipeline_mode=pl.Buffered(1))

    def weight_specs(layer_of):
        return ([layer_spec(sh, layer_of) for sh in head_shapes] + [const_spec((B_WIDTH, B_WIDTH))]
                + [layer_spec(sh, layer_of) for sh in spatial_shapes + tail_shapes])

    cparams = functools.partial(pltpu.CompilerParams, vmem_limit_bytes=VMEM_LIMIT_BYTES)

    n_slabs = batch * n_tiles
    assert all(w.shape[1] % (n_slabs * 16) == 0 for w in big_f32)
    slab_shape = lambda w, n=n_slabs: (None, w.shape[1] // n, w.shape[2])

    def cast_first_layer_kernel(*refs):
        for src, dst in zip(refs[:len(big_f32)], refs[len(big_f32):]):
            dst[...] = src[...].astype(_bf16)

    wb = pl.pallas_call(
        cast_first_layer_kernel,
        grid=(FIRST_CAST_STEPS,),
        in_specs=[pl.BlockSpec(slab_shape(w, FIRST_CAST_STEPS), lambda s: (0, s, 0)) for w in big_f32],
        out_specs=[pl.BlockSpec(slab_shape(w, FIRST_CAST_STEPS), lambda s: (0, s, 0)) for w in big_f32],
        out_shape=[jax.ShapeDtypeStruct(w.shape, _bf16) for w in big_f32],
        compiler_params=cparams(dimension_semantics=("arbitrary",)),
        name="cast_first_layer",
    )(*big_f32)

    n_fixed_in = 3 + len(head_shapes) + 1 + len(spatial_shapes) + len(tail_shapes)
    big_in_pos = {0: 4, 1: 13, 2: 15, 3: 16}

    def prompt_layer(l, xp, wb):
        cast_next = l + 1 < depth
        slab_of = lambda b, i: (l + 1, b * n_tiles + i, 0)
        in_specs = ([pl.BlockSpec(memory_space=pltpu.SMEM),
                     pl.BlockSpec((None, PROMPT_TILE, D_MODEL), lambda b, i: (b, i, 0)),
                     pl.BlockSpec((PROMPT_TILE, 3 * LANES), lambda b, i: (i, 0))]
                    + weight_specs(lambda b, i: l))
        assert len(in_specs) == n_fixed_in
        out_specs = [pl.BlockSpec((None, PROMPT_TILE, D_MODEL), lambda b, i: (b, i, 0)),
                     pl.BlockSpec((None, WINDOW, KV_COLS), lambda b, i: (b, 0, 0)),
                     pl.BlockSpec((None, WINDOW, KV_COLS), lambda b, i: (b, 0, 0))]
        out_shape = [jax.ShapeDtypeStruct((batch, seq, D_MODEL), _f32),
                     jax.ShapeDtypeStruct((batch, WINDOW, KV_COLS), _f32),
                     jax.ShapeDtypeStruct((batch, WINDOW, KV_COLS), _f32)]
        args = (sinks[l], xp, rope_p, *head(wb), ones_bd, ws.astype(_bf16), bias, *tail(wb))
        aliases = {}
        if cast_next:
            in_specs += [pl.BlockSpec(slab_shape(w), slab_of) for w in big_f32]
            out_specs += [pl.BlockSpec(slab_shape(w), slab_of) for w in big_f32]
            out_shape += [jax.ShapeDtypeStruct(w.shape, _bf16) for w in big_f32]
            args += big_f32
            aliases = {big_in_pos[j]: 3 + j for j in range(len(big_f32))}
        outs = pl.pallas_call(
            functools.partial(_prompt_kernel, n_cast=len(big_f32) if cast_next else 0),
            grid=(batch, n_tiles),
            in_specs=in_specs, out_specs=out_specs, out_shape=out_shape,
            input_output_aliases=aliases,
            scratch_shapes=[pltpu.VMEM((CHUNK, KV_COLS), _bf16),
                            pltpu.VMEM((KV_COLS, CHUNK), _bf16)],
            compiler_params=cparams(dimension_semantics=("arbitrary", "arbitrary")),
            name="prompt_layer",
        )(*args)
        return outs[0], outs[1], outs[2], (tuple(outs[3:]) if cast_next else wb)

    xp = x_prompt
    nk_p, nv_p = [], []
    for l in range(depth):
        xp, kp, vp, wb = prompt_layer(l, xp, wb)
        nk_p.append(kp)
        nv_p.append(vp)

    win_t = lambda c: jnp.transpose(c, (0, 1, 3, 4, 2)).reshape(depth, dec_batch, KV_COLS, WINDOW)
    win = lambda c: jnp.transpose(c.reshape(depth, dec_batch, KV_HEADS, HEAD_DIM, WINDOW), (0, 1, 4, 2, 3))
    layer_tile = lambda l, i: (l, i, 0, 0)
    ys, ks, vs, va_s = pl.pallas_call(
        _sample_kernel,
        grid=(depth, s_tiles),
        in_specs=[pl.BlockSpec((s_rows, D_MODEL), lambda l, i: (jnp.where(l == 0, i, s_tiles - 1), 0)),
                  const_spec((s_rows, 3 * LANES)),
                  pl.BlockSpec((None, SAMPLE_BATCH_TILE, KV_COLS, WINDOW), layer_tile),
                  pl.BlockSpec((None, SAMPLE_BATCH_TILE, KV_COLS, WINDOW), layer_tile),
                  layer_spec((KV_HEADS, GQA_GROUP * dec_seq, 1), lambda l, i: l)]
                 + weight_specs(lambda l, i: l),
        out_specs=[pl.BlockSpec((s_rows, D_MODEL), lambda l, i: (jnp.where(l == depth - 1, i, 0), 0)),
                   pl.BlockSpec((None, SAMPLE_BATCH_TILE, KV_COLS, WINDOW), layer_tile),
                   pl.BlockSpec((None, SAMPLE_BATCH_TILE, KV_COLS, WINDOW), layer_tile),
                   pl.BlockSpec((None, s_rows, A_WIDTH), lambda l, i: (l, i, 0))],
        out_shape=[jax.ShapeDtypeStruct((dec_batch * dec_seq, D_MODEL), _f32),
                   jax.ShapeDtypeStruct((depth, dec_batch, KV_COLS, WINDOW), _f32),
                   jax.ShapeDtypeStruct((depth, dec_batch, KV_COLS, WINDOW), _f32),
                   jax.ShapeDtypeStruct((depth, dec_batch * dec_seq, A_WIDTH), _f32)],
        scratch_shapes=[pltpu.VMEM((s_tiles, s_rows, D_MODEL), _f32)],
        compiler_params=cparams(dimension_semantics=("arbitrary", "arbitrary")),
        name="sample_layers",
    )(x_sample.reshape(dec_batch * dec_seq, D_MODEL), rope_s,
      win_t(cache_win_k), win_t(cache_win_v), sink_rows,
      *head(wb), ones_bd, ws_s.astype(_bf16), bias_s, *tail(wb))

    kv5 = lambda t, n: t.reshape(depth, n, WINDOW, KV_HEADS, HEAD_DIM)
    return (xp, ys.reshape(dec_batch, dec_seq, D_MODEL),
            kv5(jnp.stack(nk_p), batch), kv5(jnp.stack(nv_p), batch), win(ks), win(vs),
            va_s.reshape(depth, dec_batch, dec_seq, A_HEADS, HEAD_DIM))
```

```python
import functools

import jax
import jax.numpy as jnp
import numpy as np
from jax import lax
from jax.experimental import pallas as pl
from jax.experimental.pallas import tpu as pltpu

D_MODEL = 1024
HEAD_DIM = 64
A_WIDTH = 512
B_WIDTH = 512
A_HEADS = A_WIDTH // HEAD_DIM
N_HEADS = B_WIDTH // HEAD_DIM
KV_HEADS = 2
GQA_GROUP = N_HEADS // KV_HEADS
KV_COLS = KV_HEADS * HEAD_DIM
CHUNK = 128
WINDOW = 128
ROPE_THETA = 500000.0
ROT_DIM = HEAD_DIM // 4
D_FF = 4 * D_MODEL
EPS = 1e-6
IN_COLS = 2 * A_WIDTH + B_WIDTH + 2 * KV_COLS
Q_OFF = 2 * A_WIDTH
K_OFF = Q_OFF + B_WIDTH
V_OFF = K_OFF + KV_COLS
NEG_BIG = -1e30
LOG2_E = float(np.log2(np.e))

LANES = 128
FF_CHUNK = 1024
PROMPT_TILE = 512
SAMPLE_BATCH_TILE = 16
SAMPLE_ATTN_GROUP = 1
FIRST_CAST_STEPS = 8
VMEM_LIMIT_BYTES = 56 * 1024 * 1024

_bf16 = jnp.bfloat16
_f32 = jnp.float32


def _dot(a, b):
    return jnp.dot(a, b, preferred_element_type=_f32)


def _rms(x, g):
    return x * lax.rsqrt(jnp.mean(x * x, axis=-1, keepdims=True) + EPS) * g


def _gelu(x):
    return 0.5 * x * (1.0 + lax.erf(x * np.float32(np.sqrt(0.5))))


def _head_rms(z, g_tiled, ones_bd):
    ss = _dot((z * z).astype(_bf16), ones_bd)
    return z * lax.rsqrt(ss * (1.0 / HEAD_DIM) + EPS) * g_tiled


def _rope(x, cos_t, sin_a, sin_b):
    cols = []
    for c in range(x.shape[1] // LANES):
        xc = x[:, c * LANES:(c + 1) * LANES]
        nxt = pltpu.roll(xc, LANES - ROT_DIM // 2, 1)
        prv = pltpu.roll(xc, ROT_DIM // 2, 1)
        cols.append(xc * cos_t + nxt * sin_a + prv * sin_b)
    return cols[0] if len(cols) == 1 else jnp.concatenate(cols, axis=1)


def _in_proj(x, g_mix, w_in):
    return _dot(_rms(x, g_mix[...]).astype(_bf16), w_in[...])


def _split_inputs(z, rope_ref, g_sv, g_q, g_k, ones_bd):
    u = _gelu(z[:, :A_WIDTH])
    va = _rms(_gelu(z[:, A_WIDTH:2 * A_WIDTH]), g_sv[...])
    cos_t = rope_ref[:, 0:LANES]
    sin_a = rope_ref[:, LANES:2 * LANES]
    sin_b = rope_ref[:, 2 * LANES:3 * LANES]
    ones = ones_bd[...]
    q = _rope(_head_rms(z[:, Q_OFF:K_OFF], g_q[...], ones), cos_t, sin_a, sin_b)
    k = _rope(_head_rms(z[:, K_OFF:V_OFF], g_k[...], ones[:KV_COLS, :KV_COLS]), cos_t, sin_a, sin_b)
    v = z[:, V_OFF:]
    return u, va, q, k, v


def _spatial(va_blk, u_blk, ws_ref, bias):
    vb = va_blk.astype(_bf16)
    outs = [_dot(ws_ref[h], vb[:, h * HEAD_DIM:(h + 1) * HEAD_DIM]) for h in range(A_HEADS)]
    return u_blk * (jnp.concatenate(outs, axis=1) + bias)


def _merge(x, ya, yb, g_oa, g_ob, w_out, g_ffn):
    cat = jnp.concatenate([_rms(ya, g_oa[...]), _rms(yb, g_ob[...])], axis=1).astype(_bf16)
    x = x + _dot(cat, w_out[...])
    return x, _rms(x, g_ffn[...]).astype(_bf16)


def _ffn_chunk(acc, hn, w_up, w_down, c, width):
    h = _dot(hn, w_up[:, c * width:(c + 1) * width])
    h = jnp.maximum(h, 0.0)
    return acc + _dot((h * h).astype(_bf16), w_down[c * width:(c + 1) * width, :])


def _ffn(x, hn, w_up, w_down):
    acc = x
    for c in range(D_FF // FF_CHUNK):
        acc = _ffn_chunk(acc, hn, w_up, w_down, c, FF_CHUNK)
    return acc


def _dot_nt(a, b):
    return lax.dot_general(a, b, (((1,), (1,)), ((), ())), preferred_element_type=_f32)


def _swap_halves(t):
    return pltpu.roll(t, HEAD_DIM, 1)


def _lane_lo(shape):
    return lax.broadcasted_iota(jnp.int32, shape, len(shape) - 1) < HEAD_DIM


def _prompt_kernel(sinks_ref, x_ref, rope_ref, g_mix, w_in, g_sv, g_q, g_k, ones_bd, ws_ref,
                   bias_ref, g_oa, g_ob, w_out, g_ffn, w_up, w_down, *rest, n_cast):
    cast_src, rest = rest[:n_cast], rest[n_cast:]
    y_ref, ks_ref, vs_ref = rest[:3]
    cast_dst, (kprev, vprev) = rest[3:3 + n_cast], rest[3 + n_cast:]
    for src, dst in zip(cast_src, cast_dst):
        dst[...] = src[...].astype(_bf16)

    i = pl.program_id(1)
    tile = x_ref.shape[0]
    n_blk = tile // CHUNK

    @pl.when(i == 0)
    def _():
        kprev[...] = jnp.zeros_like(kprev)
        vprev[...] = jnp.zeros_like(vprev)

    x = x_ref[...]
    u, va, q, k, v = _split_inputs(_in_proj(x, g_mix, w_in), rope_ref, g_sv, g_q, g_k, ones_bd)
    ks_ref[...] = k[tile - WINDOW:, :]
    vs_ref[...] = v[tile - WINDOW:, :]

    bias = bias_ref[...]
    row = lax.broadcasted_iota(jnp.int32, (CHUNK, 2 * CHUNK), 0)
    col = lax.broadcasted_iota(jnp.int32, (CHUNK, 2 * CHUNK), 1)
    lo = _lane_lo((CHUNK, LANES))
    qs = q * (HEAD_DIM ** -0.5 * LOG2_E)

    ya_blocks, yb_blocks = [], []
    for blk in range(n_blk):
        r0, r1 = blk * CHUNK, (blk + 1) * CHUNK
        ya_blocks.append(_spatial(va[r0:r1], u[r0:r1], ws_ref, bias))

        k_blk, v_blk = k[r0:r1], v[r0:r1]
        k_new = (k_blk.astype(_bf16), _swap_halves(k_blk).astype(_bf16))
        v_new = (v_blk.astype(_bf16), _swap_halves(v_blk).astype(_bf16))
        if blk == 0:
            k_old = (kprev[0], kprev[1])
            v_old = (vprev[0], vprev[1])
            first = jnp.where(i == 0, 2 * CHUNK, 0)
        else:
            first = 0
        valid = jnp.where(col < CHUNK, col - row - first, row + CHUNK - col) >= 0
        k_ctx = [jnp.concatenate([k_old[a], k_new[a]], axis=0) for a in range(2)]
        v_ctx = [jnp.concatenate([v_old[a], v_new[a]], axis=0) for a in range(2)]
        k_old, v_old = k_new, v_new

        pairs = []
        for p in range(N_HEADS // 2):
            q_pair = qs[r0:r1, p * LANES:(p + 1) * LANES]
            halves = []
            for half in range(2):
                h = 2 * p + half
                kv = h // GQA_GROUP
                src = (kv + half) % 2
                qm = jnp.where(lo if half == 0 else ~lo, q_pair, 0.0).astype(_bf16)
                sc = jnp.where(valid, _dot_nt(qm, k_ctx[src]), NEG_BIG)
                sink = sinks_ref[h] * LOG2_E
                m = jnp.maximum(jnp.max(sc, axis=-1, keepdims=True), sink)
                e = jnp.exp2(sc - m)
                denom = jnp.sum(e, axis=-1, keepdims=True) + jnp.exp2(sink - m)
                o = _dot(e.astype(_bf16), v_ctx[src])
                halves.append(o * (1.0 / denom))
            pairs.append(jnp.where(lo, halves[0], halves[1]))
        yb_blocks.append(jnp.concatenate(pairs, axis=1))

    kprev[0], kprev[1] = k_old
    vprev[0], vprev[1] = v_old
    ya = jnp.concatenate(ya_blocks, axis=0)
    yb = jnp.concatenate(yb_blocks, axis=0)
    x_new, hn = _merge(x, ya, yb, g_oa, g_ob, w_out, g_ffn)
    y_ref[...] = _ffn(x_new, hn, w_up, w_down)


def _sample_kernel(x_ref, rope_ref, ck_ref, cv_ref, sink_rows, g_mix, w_in, g_sv, g_q, g_k, ones_bd,
                   ws_ref, bias_ref, g_oa, g_ob, w_out, g_ffn, w_up, w_down,
                   y_ref, ko_ref, vo_ref, va_ref, x_all):
    l = pl.program_id(0)
    i = pl.program_id(1)
    rows = x_ref.shape[0]
    nb = ck_ref.shape[0]
    dec_seq = rows // nb
    grp_rows = GQA_GROUP * dec_seq

    @pl.when(l == 0)
    def _():
        x_all[i] = x_ref[...]

    x = x_all[i]
    u, va, q, k, v = _split_inputs(_in_proj(x, g_mix, w_in), rope_ref, g_sv, g_q, g_k, ones_bd)
    va_ref[...] = va

    bias = bias_ref[...]
    ya = jnp.concatenate(
        [_spatial(va[r:r + CHUNK], u[r:r + CHUNK], ws_ref, bias) for r in range(0, rows, CHUNK)], axis=0)

    ck, cv = ck_ref[...], cv_ref[...]
    k_t, v_t = k.T, v.T
    lane = lax.broadcasted_iota(jnp.int32, (KV_COLS, WINDOW), 1)
    for b in range(nb):
        col = b * dec_seq
        grp, off = col // LANES, col % LANES
        for new_t, cache, out_ref in ((k_t, ck, ko_ref), (v_t, cv, vo_ref)):
            fresh = pltpu.roll(new_t[:, grp * LANES:(grp + 1) * LANES], (LANES - off) % LANES, 1)
            merged = jnp.where(lane < dec_seq, fresh, cache[b])
            out_ref[b] = pltpu.roll(merged, WINDOW - dec_seq, 1)

    gb = SAMPLE_ATTN_GROUP
    ng = nb // gb
    g_rows = gb * grp_rows
    ck16, cv16 = ck.astype(_bf16), cv.astype(_bf16)
    kt16 = k_t.astype(_bf16)
    q3 = (q * (HEAD_DIM ** -0.5)).reshape(nb, dec_seq, B_WIDTH)
    r_c = lax.broadcasted_iota(jnp.int32, (g_rows, gb * WINDOW), 0)
    c_c = lax.broadcasted_iota(jnp.int32, (g_rows, gb * WINDOW), 1)
    w_own = c_c - (r_c - r_c % grp_rows) * (WINDOW // grp_rows)
    valid_c = (jnp.minimum(w_own - r_c % dec_seq, WINDOW - 1 - w_own) >= 0)[None]
    g_n = lax.broadcasted_iota(jnp.int32, (ng, g_rows, rows), 0)
    r_n = lax.broadcasted_iota(jnp.int32, (ng, g_rows, rows), 1)
    c_n = lax.broadcasted_iota(jnp.int32, (ng, g_rows, rows), 2)
    t_key = c_n * GQA_GROUP - (g_n * g_rows + r_n - r_n % grp_rows)
    valid_n = jnp.minimum(t_key, (r_n % dec_seq) * GQA_GROUP - t_key) >= 0

    def side_by_side(t16, d0, d1):
        return jnp.stack([jnp.concatenate([t16[g * gb + j, d0:d1, :] for j in range(gb)], axis=1)
                          for g in range(ng)])

    heads = []
    for kv in range(KV_HEADS):
        d0, d1 = kv * HEAD_DIM, (kv + 1) * HEAD_DIM
        q_st = jnp.concatenate(
            [q3[:, :, h * HEAD_DIM:(h + 1) * HEAD_DIM] for h in range(kv * GQA_GROUP, (kv + 1) * GQA_GROUP)],
            axis=1).astype(_bf16)
        q_g = q_st.reshape(ng, g_rows, HEAD_DIM)
        s_c = jnp.einsum('gqd,gdw->gqw', q_g, side_by_side(ck16, d0, d1), preferred_element_type=_f32)
        s_n = _dot(q_st.reshape(nb * grp_rows, HEAD_DIM), kt16[d0:d1, :]).reshape(ng, g_rows, rows)
        s_c = jnp.where(valid_c, s_c, NEG_BIG)
        s_n = jnp.where(valid_n, s_n, NEG_BIG)
        sink = jnp.concatenate([sink_rows[kv]] * gb, axis=0)[None]
        m = jnp.maximum(jnp.maximum(jnp.max(s_c, axis=-1, keepdims=True),
                                    jnp.max(s_n, axis=-1, keepdims=True)), sink)
        e_c, e_n = jnp.exp(s_c - m), jnp.exp(s_n - m)
        denom = (jnp.sum(e_c, axis=-1, keepdims=True) + jnp.sum(e_n, axis=-1, keepdims=True)
                 + jnp.exp(sink - m))
        o = jnp.einsum('gqw,gdw->gqd', e_c.astype(_bf16), side_by_side(cv16, d0, d1),
                       preferred_element_type=_f32)
        o = o + _dot(e_n.reshape(nb * grp_rows, rows).astype(_bf16),
                     v[:, d0:d1].astype(_bf16)).reshape(ng, g_rows, HEAD_DIM)
        o = (o * (1.0 / denom)).reshape(nb, grp_rows, HEAD_DIM)
        heads += [o[:, g * dec_seq:(g + 1) * dec_seq, :] for g in range(GQA_GROUP)]
    yb = jnp.concatenate(heads, axis=2).reshape(rows, B_WIDTH)
    x_new, hn = _merge(x, ya, yb, g_oa, g_ob, w_out, g_ffn)
    y = _ffn(x_new, hn, w_up, w_down)
    x_all[i] = y
    y_ref[...] = y


def _rope_tables(pos):
    half = ROT_DIM // 2
    inv = np.power(np.float32(ROPE_THETA), -2.0 * np.arange(half, dtype=np.float32) / np.float32(ROT_DIM))
    ang = pos.astype(np.float32)[:, None] * inv.astype(np.float32)[None, :]
    cos, sin = np.cos(ang).astype(np.float32), np.sin(ang).astype(np.float32)
    n = pos.shape[0]
    pad1 = np.ones((n, HEAD_DIM - ROT_DIM), np.float32)
    pad0 = np.zeros((n, HEAD_DIM - ROT_DIM), np.float32)
    zero = np.zeros((n, half), np.float32)
    cos_t = np.concatenate([cos, cos, pad1], axis=1)
    sin_a = np.concatenate([-sin, zero, pad0], axis=1)
    sin_b = np.concatenate([zero, sin, pad0], axis=1)
    rep = LANES // HEAD_DIM
    return np.concatenate([np.tile(t, (1, rep)) for t in (cos_t, sin_a, sin_b)], axis=1)


def kernel(x_prompt, x_sample, cache_win_k, cache_win_v, g_mix, w_in, g_sv, g_q, g_k, w_spatial,
           b_spatial, sinks, g_out_a, g_out_b, w_out, g_ffn, w_up, w_down):
    batch, seq, _ = x_prompt.shape
    dec_batch, dec_seq, _ = x_sample.shape
    depth = w_in.shape[0]
    assert seq % PROMPT_TILE == 0 and PROMPT_TILE % CHUNK == 0
    assert dec_batch % SAMPLE_BATCH_TILE == 0 and CHUNK % dec_seq == 0
    s_rows = SAMPLE_BATCH_TILE * dec_seq
    assert s_rows % CHUNK == 0 and SAMPLE_BATCH_TILE % SAMPLE_ATTN_GROUP == 0
    assert WINDOW % (GQA_GROUP * dec_seq) == 0
    n_tiles = seq // PROMPT_TILE
    s_tiles = dec_batch // SAMPLE_BATCH_TILE

    rope_p = jnp.asarray(_rope_tables(np.arange(seq)))
    rope_s = jnp.asarray(np.tile(_rope_tables(seq + np.arange(dec_seq)), (SAMPLE_BATCH_TILE, 1)))
    hid = np.arange(B_WIDTH) // HEAD_DIM
    ones_bd = jnp.asarray(hid[:, None] == hid[None, :], dtype=_bf16)
    tril = np.tril(np.ones((CHUNK, CHUNK), dtype=bool))
    rep = CHUNK // dec_seq
    expand = jnp.asarray(np.tile(np.eye(dec_seq, dtype=np.float32), (rep, 1)))
    same_batch = jnp.asarray(np.kron(np.eye(rep), np.ones((dec_seq, dec_seq))), dtype=_f32)

    row3 = lambda a: a[:, None, :]
    ws = jnp.where(tril, w_spatial, 0.0)
    bias = jnp.repeat(jnp.swapaxes(b_spatial, 1, 2), HEAD_DIM, axis=2)
    ws_s = jnp.einsum('rt,lhts,cs->lhrc', expand, ws[:, :, :dec_seq, :dec_seq], expand,
                      precision=lax.Precision.HIGHEST) * same_batch
    bias_s = jnp.tile(bias[:, :dec_seq], (1, rep, 1))
    sink_rows = jnp.repeat(sinks.reshape(depth, KV_HEADS, GQA_GROUP), dec_seq, axis=2)[..., None]
    big_f32 = (w_in, w_out, w_up, w_down)
    head = lambda wb: (row3(g_mix), wb[0], row3(g_sv), row3(jnp.tile(g_q, (1, N_HEADS))),
                       row3(jnp.tile(g_k, (1, KV_HEADS))))
    tail = lambda wb: (row3(g_out_a), row3(g_out_b), wb[1], row3(g_ffn), wb[2], wb[3])
    head_shapes = [(1, D_MODEL), (D_MODEL, IN_COLS), (1, A_WIDTH), (1, B_WIDTH), (1, KV_COLS)]
    tail_shapes = [(1, A_WIDTH), (1, B_WIDTH), (D_MODEL, D_MODEL), (1, D_MODEL), (D_MODEL, D_FF), (D_FF, D_MODEL)]
    spatial_shapes = [(A_HEADS, CHUNK, CHUNK), (CHUNK, A_WIDTH)]

    def const_spec(shape):
        nd = len(shape)
        return pl.BlockSpec(shape, lambda *_: (0,) * nd, pipeline_mode=pl.Buffered(1))

    def layer_spec(shape, layer_of):
        nd = len(shape)
        return pl.BlockSpec((None,) + tuple(shape), lambda *g: (layer_of(*g),) + (0,) * nd,
                            pipeline_mode=pl.Buffered(1))

    def weight_specs(layer_of):
        return ([layer_spec(sh, layer_of) for sh in head_shapes] + [const_spec((B_WIDTH, B_WIDTH))]
                + [layer_spec(sh, layer_of) for sh in spatial_shapes + tail_shapes])

    cparams = functools.partial(pltpu.CompilerParams, vmem_limit_bytes=VMEM_LIMIT_BYTES)

    n_slabs = batch * n_tiles
    assert all(w.shape[1] % (n_slabs * 16) == 0 for w in big_f32)
    slab_shape = lambda w, n=n_slabs: (None, w.shape[1] // n, w.shape[2])

    def cast_first_layer_kernel(*refs):
        for src, dst in zip(refs[:len(big_f32)], refs[len(big_f32):]):
            dst[...] = src[...].astype(_bf16)

    wb = pl.pallas_call(
        cast_first_layer_kernel,
        grid=(FIRST_CAST_STEPS,),
        in_specs=[pl.BlockSpec(slab_shape(w, FIRST_CAST_STEPS), lambda s: (0, s, 0)) for w in big_f32],
        out_specs=[pl.BlockSpec(slab_shape(w, FIRST_CAST_STEPS), lambda s: (0, s, 0)) for w in big_f32],
        out_shape=[jax.ShapeDtypeStruct(w.shape, _bf16) for w in big_f32],
        compiler_params=cparams(dimension_semantics=("arbitrary",)),
        name="cast_first_layer",
    )(*big_f32)

    n_fixed_in = 3 + len(head_shapes) + 1 + len(spatial_shapes) + len(tail_shapes)
    big_in_pos = {0: 4, 1: 13, 2: 15, 3: 16}

    def prompt_layer(l, xp, wb):
        cast_next = l + 1 < depth
        slab_of = lambda b, i: (l + 1, b * n_tiles + i, 0)
        in_specs = ([pl.BlockSpec(memory_space=pltpu.SMEM),
                     pl.BlockSpec((None, PROMPT_TILE, D_MODEL), lambda b, i: (b, i, 0)),
                     pl.BlockSpec((PROMPT_TILE, 3 * LANES), lambda b, i: (i, 0))]
                    + weight_specs(lambda b, i: l))
        assert len(in_specs) == n_fixed_in
        out_specs = [pl.BlockSpec((None, PROMPT_TILE, D_MODEL), lambda b, i: (b, i, 0)),
                     pl.BlockSpec((None, WINDOW, KV_COLS), lambda b, i: (b, 0, 0)),
                     pl.BlockSpec((None, WINDOW, KV_COLS), lambda b, i: (b, 0, 0))]
        out_shape = [jax.ShapeDtypeStruct((batch, seq, D_MODEL), _f32),
                     jax.ShapeDtypeStruct((batch, WINDOW, KV_COLS), _f32),
                     jax.ShapeDtypeStruct((batch, WINDOW, KV_COLS), _f32)]
        args = (sinks[l], xp, rope_p, *head(wb), ones_bd, ws.astype(_bf16), bias, *tail(wb))
        aliases = {}
        if cast_next:
            in_specs += [pl.BlockSpec(slab_shape(w), slab_of) for w in big_f32]
            out_specs += [pl.BlockSpec(slab_shape(w), slab_of) for w in big_f32]
            out_shape += [jax.ShapeDtypeStruct(w.shape, _bf16) for w in big_f32]
            args += big_f32
            aliases = {big_in_pos[j]: 3 + j for j in range(len(big_f32))}
        outs = pl.pallas_call(
            functools.partial(_prompt_kernel, n_cast=len(big_f32) if cast_next else 0),
            grid=(batch, n_tiles),
            in_specs=in_specs, out_specs=out_specs, out_shape=out_shape,
            input_output_aliases=aliases,
            scratch_shapes=[pltpu.VMEM((2, CHUNK, KV_COLS), _bf16), pltpu.VMEM((2, CHUNK, KV_COLS), _bf16)],
            compiler_params=cparams(dimension_semantics=("arbitrary", "arbitrary")),
            name="prompt_layer",
        )(*args)
        return outs[0], outs[1], outs[2], (tuple(outs[3:]) if cast_next else wb)

    xp = x_prompt
    nk_p, nv_p = [], []
    for l in range(depth):
        xp, kp, vp, wb = prompt_layer(l, xp, wb)
        nk_p.append(kp)
        nv_p.append(vp)

    win_t = lambda c: jnp.transpose(c, (0, 1, 3, 4, 2)).reshape(depth, dec_batch, KV_COLS, WINDOW)
    win = lambda c: jnp.transpose(c.reshape(depth, dec_batch, KV_HEADS, HEAD_DIM, WINDOW), (0, 1, 4, 2, 3))
    layer_tile = lambda l, i: (l, i, 0, 0)
    ys, ks, vs, va_s = pl.pallas_call(
        _sample_kernel,
        grid=(depth, s_tiles),
        in_specs=[pl.BlockSpec((s_rows, D_MODEL), lambda l, i: (jnp.where(l == 0, i, s_tiles - 1), 0)),
                  const_spec((s_rows, 3 * LANES)),
                  pl.BlockSpec((None, SAMPLE_BATCH_TILE, KV_COLS, WINDOW), layer_tile),
                  pl.BlockSpec((None, SAMPLE_BATCH_TILE, KV_COLS, WINDOW), layer_tile),
                  layer_spec((KV_HEADS, GQA_GROUP * dec_seq, 1), lambda l, i: l)]
                 + weight_specs(lambda l, i: l),
        out_specs=[pl.BlockSpec((s_rows, D_MODEL), lambda l, i: (jnp.where(l == depth - 1, i, 0), 0)),
                   pl.BlockSpec((None, SAMPLE_BATCH_TILE, KV_COLS, WINDOW), layer_tile),
                   pl.BlockSpec((None, SAMPLE_BATCH_TILE, KV_COLS, WINDOW), layer_tile),
                   pl.BlockSpec((None, s_rows, A_WIDTH), lambda l, i: (l, i, 0))],
        out_shape=[jax.ShapeDtypeStruct((dec_batch * dec_seq, D_MODEL), _f32),
                   jax.ShapeDtypeStruct((depth, dec_batch, KV_COLS, WINDOW), _f32),
                   jax.ShapeDtypeStruct((depth, dec_batch, KV_COLS, WINDOW), _f32),
                   jax.ShapeDtypeStruct((depth, dec_batch * dec_seq, A_WIDTH), _f32)],
        scratch_shapes=[pltpu.VMEM((s_tiles, s_rows, D_MODEL), _f32)],
        compiler_params=cparams(dimension_semantics=("arbitrary", "arbitrary")),
        name="sample_layers",
    )(x_sample.reshape(dec_batch * dec_seq, D_MODEL), rope_s,
      win_t(cache_win_k), win_t(cache_win_v), sink_rows,
      *head(wb), ones_bd, ws_s.astype(_bf16), bias_s, *tail(wb))

    kv5 = lambda t, n: t.reshape(depth, n, WINDOW, KV_HEADS, HEAD_DIM)
    return (xp, ys.reshape(dec_batch, dec_seq, D_MODEL),
            kv5(jnp.stack(nk_p), batch), kv5(jnp.stack(nv_p), batch), win(ks), win(vs),
            va_s.reshape(depth, dec_batch, dec_seq, A_HEADS, HEAD_DIM))
```

```python
import functools

import jax
import jax.numpy as jnp
import numpy as np
from jax import lax
from jax.experimental import pallas as pl
from jax.experimental.pallas import tpu as pltpu

D_MODEL = 1024
HEAD_DIM = 64
A_WIDTH = 512
B_WIDTH = 512
A_HEADS = A_WIDTH // HEAD_DIM
N_HEADS = B_WIDTH // HEAD_DIM
KV_HEADS = 2
GQA_GROUP = N_HEADS // KV_HEADS
KV_COLS = KV_HEADS * HEAD_DIM
CHUNK = 128
WINDOW = 128
ROPE_THETA = 500000.0
ROT_DIM = HEAD_DIM // 4
D_FF = 4 * D_MODEL
EPS = 1e-6
IN_COLS = 2 * A_WIDTH + B_WIDTH + 2 * KV_COLS
Q_OFF = 2 * A_WIDTH
K_OFF = Q_OFF + B_WIDTH
V_OFF = K_OFF + KV_COLS
NEG_BIG = -1e30
LOG2_E = float(np.log2(np.e))

LANES = 128
FF_CHUNK = 1024
PROMPT_TILE = 512
SAMPLE_BATCH_TILE = 16
SAMPLE_ATTN_GROUP = 1
SAMPLE_PREFETCH_SHAPES = ((D_MODEL, IN_COLS), (D_MODEL, D_MODEL), (D_MODEL, D_FF))
FIRST_CAST_STEPS = 8
VMEM_LIMIT_BYTES = 56 * 1024 * 1024

_bf16 = jnp.bfloat16
_f32 = jnp.float32


def _dot(a, b):
    return jnp.dot(a, b, preferred_element_type=_f32)


def _rms(x, g):
    return x * lax.rsqrt(jnp.mean(x * x, axis=-1, keepdims=True) + EPS) * g


def _gelu(x):
    return 0.5 * x * (1.0 + lax.erf(x * np.float32(np.sqrt(0.5))))


def _head_rms(z, g_tiled, ones_bd):
    ss = _dot((z * z).astype(_bf16), ones_bd)
    return z * lax.rsqrt(ss * (1.0 / HEAD_DIM) + EPS) * g_tiled


def _rope(x, cos_t, sin_a, sin_b):
    cols = []
    for c in range(x.shape[1] // LANES):
        xc = x[:, c * LANES:(c + 1) * LANES]
        nxt = pltpu.roll(xc, LANES - ROT_DIM // 2, 1)
        prv = pltpu.roll(xc, ROT_DIM // 2, 1)
        cols.append(xc * cos_t + nxt * sin_a + prv * sin_b)
    return cols[0] if len(cols) == 1 else jnp.concatenate(cols, axis=1)


def _in_proj(x, g_mix, w_in):
    return _dot(_rms(x, g_mix[...]).astype(_bf16), w_in[...])


def _split_inputs(z, rope_ref, g_sv, g_q, g_k, ones_bd):
    u = _gelu(z[:, :A_WIDTH])
    va = _rms(_gelu(z[:, A_WIDTH:2 * A_WIDTH]), g_sv[...])
    cos_t = rope_ref[:, 0:LANES]
    sin_a = rope_ref[:, LANES:2 * LANES]
    sin_b = rope_ref[:, 2 * LANES:3 * LANES]
    ones = ones_bd[...]
    q = _rope(_head_rms(z[:, Q_OFF:K_OFF], g_q[...], ones), cos_t, sin_a, sin_b)
    k = _rope(_head_rms(z[:, K_OFF:V_OFF], g_k[...], ones[:KV_COLS, :KV_COLS]), cos_t, sin_a, sin_b)
    v = z[:, V_OFF:]
    return u, va, q, k, v


def _spatial(va_blk, u_blk, ws_ref, bias):
    vb = va_blk.astype(_bf16)
    outs = [_dot(ws_ref[h], vb[:, h * HEAD_DIM:(h + 1) * HEAD_DIM]) for h in range(A_HEADS)]
    return u_blk * (jnp.concatenate(outs, axis=1) + bias)


def _merge(x, ya, yb, g_oa, g_ob, w_out, g_ffn):
    cat = jnp.concatenate([_rms(ya, g_oa[...]), _rms(yb, g_ob[...])], axis=1).astype(_bf16)
    x = x + _dot(cat, w_out[...])
    return x, _rms(x, g_ffn[...]).astype(_bf16)


def _ffn_chunk(acc, hn, w_up, w_down, c, width):
    h = _dot(hn, w_up[:, c * width:(c + 1) * width])
    h = jnp.maximum(h, 0.0)
    return acc + _dot((h * h).astype(_bf16), w_down[c * width:(c + 1) * width, :])


def _ffn(x, hn, w_up, w_down):
    acc = x
    for c in range(D_FF // FF_CHUNK):
        acc = _ffn_chunk(acc, hn, w_up, w_down, c, FF_CHUNK)
    return acc


def _dot_nt(a, b):
    return lax.dot_general(a, b, (((1,), (1,)), ((), ())), preferred_element_type=_f32)


def _swap_halves(t):
    return pltpu.roll(t, HEAD_DIM, 1)


def _lane_lo(shape):
    return lax.broadcasted_iota(jnp.int32, shape, len(shape) - 1) < HEAD_DIM


def _prompt_kernel(sinks_ref, x_ref, rope_ref, g_mix, w_in, g_sv, g_q, g_k, ones_bd, ws_ref,
                   bias_ref, g_oa, g_ob, w_out, g_ffn, w_up, w_down, *rest, n_cast):
    cast_src, rest = rest[:n_cast], rest[n_cast:]
    y_ref, ks_ref, vs_ref = rest[:3]
    cast_dst, (kprev, vprev) = rest[3:3 + n_cast], rest[3 + n_cast:]
    for src, dst in zip(cast_src, cast_dst):
        dst[...] = src[...].astype(_bf16)

    i = pl.program_id(1)
    tile = x_ref.shape[0]
    n_blk = tile // CHUNK

    @pl.when(i == 0)
    def _():
        kprev[...] = jnp.zeros_like(kprev)
        vprev[...] = jnp.zeros_like(vprev)

    x = x_ref[...]
    u, va, q, k, v = _split_inputs(_in_proj(x, g_mix, w_in), rope_ref, g_sv, g_q, g_k, ones_bd)
    ks_ref[...] = k[tile - WINDOW:, :]
    vs_ref[...] = v[tile - WINDOW:, :]

    bias = bias_ref[...]
    row = lax.broadcasted_iota(jnp.int32, (CHUNK, 2 * CHUNK), 0)
    col = lax.broadcasted_iota(jnp.int32, (CHUNK, 2 * CHUNK), 1)
    lo = _lane_lo((CHUNK, LANES))
    qs = q * (HEAD_DIM ** -0.5 * LOG2_E)

    ya_blocks, yb_blocks = [], []
    for blk in range(n_blk):
        r0, r1 = blk * CHUNK, (blk + 1) * CHUNK
        ya_blocks.append(_spatial(va[r0:r1], u[r0:r1], ws_ref, bias))

        k_blk, v_blk = k[r0:r1], v[r0:r1]
        k_new = (k_blk.astype(_bf16), _swap_halves(k_blk).astype(_bf16))
        v_new = (v_blk.astype(_bf16), _swap_halves(v_blk).astype(_bf16))
        if blk == 0:
            k_old = (kprev[0], kprev[1])
            v_old = (vprev[0], vprev[1])
            first = jnp.where(i == 0, 2 * CHUNK, 0)
        else:
            first = 0
        valid = jnp.where(col < CHUNK, col - row - first, row + CHUNK - col) >= 0
        k_ctx = [jnp.concatenate([k_old[a], k_new[a]], axis=0) for a in range(2)]
        v_ctx = [jnp.concatenate([v_old[a], v_new[a]], axis=0) for a in range(2)]
        k_old, v_old = k_new, v_new

        pairs = []
        for p in range(N_HEADS // 2):
            q_pair = qs[r0:r1, p * LANES:(p + 1) * LANES]
            halves = []
            for half in range(2):
                h = 2 * p + half
                kv = h // GQA_GROUP
                src = (kv + half) % 2
                qm = jnp.where(lo if half == 0 else ~lo, q_pair, 0.0).astype(_bf16)
                sc = jnp.where(valid, _dot_nt(qm, k_ctx[src]), NEG_BIG)
                sink = sinks_ref[h] * LOG2_E
                m = jnp.maximum(jnp.max(sc, axis=-1, keepdims=True), sink)
                e = jnp.exp2(sc - m)
                denom = jnp.sum(e, axis=-1, keepdims=True) + jnp.exp2(sink - m)
                o = _dot(e.astype(_bf16), v_ctx[src])
                halves.append(o * (1.0 / denom))
            pairs.append(jnp.where(lo, halves[0], halves[1]))
        yb_blocks.append(jnp.concatenate(pairs, axis=1))

    kprev[0], kprev[1] = k_old
    vprev[0], vprev[1] = v_old
    ya = jnp.concatenate(ya_blocks, axis=0)
    yb = jnp.concatenate(yb_blocks, axis=0)
    x_new, hn = _merge(x, ya, yb, g_oa, g_ob, w_out, g_ffn)
    y_ref[...] = _ffn(x_new, hn, w_up, w_down)


def _sample_kernel(x_ref, rope_ref, ck_ref, cv_ref, sink_rows, g_mix, w_in, g_sv, g_q, g_k, ones_bd,
                   ws_ref, bias_ref, g_oa, g_ob, w_out, g_ffn, w_up, w_down,
                   y_ref, ko_ref, vo_ref, va_ref, x_all):
    l = pl.program_id(0)
    i = pl.program_id(1)
    rows = x_ref.shape[0]
    nb = ck_ref.shape[0]
    dec_seq = rows // nb
    grp_rows = GQA_GROUP * dec_seq

    @pl.when(l == 0)
    def _():
        x_all[i] = x_ref[...]

    x = x_all[i]
    u, va, q, k, v = _split_inputs(_in_proj(x, g_mix, w_in), rope_ref, g_sv, g_q, g_k, ones_bd)
    va_ref[...] = va

    bias = bias_ref[...]
    ya = jnp.concatenate(
        [_spatial(va[r:r + CHUNK], u[r:r + CHUNK], ws_ref, bias) for r in range(0, rows, CHUNK)], axis=0)

    ck, cv = ck_ref[...], cv_ref[...]
    k_t, v_t = k.T, v.T
    lane = lax.broadcasted_iota(jnp.int32, (KV_COLS, WINDOW), 1)
    for b in range(nb):
        col = b * dec_seq
        grp, off = col // LANES, col % LANES
        for new_t, cache, out_ref in ((k_t, ck, ko_ref), (v_t, cv, vo_ref)):
            fresh = pltpu.roll(new_t[:, grp * LANES:(grp + 1) * LANES], (LANES - off) % LANES, 1)
            merged = jnp.where(lane < dec_seq, fresh, cache[b])
            out_ref[b] = pltpu.roll(merged, WINDOW - dec_seq, 1)

    gb = SAMPLE_ATTN_GROUP
    ng = nb // gb
    g_rows = gb * grp_rows
    ck16, cv16 = ck.astype(_bf16), cv.astype(_bf16)
    kt16 = k_t.astype(_bf16)
    q3 = (q * (HEAD_DIM ** -0.5)).reshape(nb, dec_seq, B_WIDTH)
    r_c = lax.broadcasted_iota(jnp.int32, (g_rows, gb * WINDOW), 0)
    c_c = lax.broadcasted_iota(jnp.int32, (g_rows, gb * WINDOW), 1)
    w_own = c_c - (r_c - r_c % grp_rows) * (WINDOW // grp_rows)
    valid_c = (jnp.minimum(w_own - r_c % dec_seq, WINDOW - 1 - w_own) >= 0)[None]
    g_n = lax.broadcasted_iota(jnp.int32, (ng, g_rows, rows), 0)
    r_n = lax.broadcasted_iota(jnp.int32, (ng, g_rows, rows), 1)
    c_n = lax.broadcasted_iota(jnp.int32, (ng, g_rows, rows), 2)
    t_key = c_n * GQA_GROUP - (g_n * g_rows + r_n - r_n % grp_rows)
    valid_n = jnp.minimum(t_key, (r_n % dec_seq) * GQA_GROUP - t_key) >= 0

    def side_by_side(t16, d0, d1):
        return jnp.stack([jnp.concatenate([t16[g * gb + j, d0:d1, :] for j in range(gb)], axis=1)
                          for g in range(ng)])

    heads = []
    for kv in range(KV_HEADS):
        d0, d1 = kv * HEAD_DIM, (kv + 1) * HEAD_DIM
        q_st = jnp.concatenate(
            [q3[:, :, h * HEAD_DIM:(h + 1) * HEAD_DIM] for h in range(kv * GQA_GROUP, (kv + 1) * GQA_GROUP)],
            axis=1).astype(_bf16)
        q_g = q_st.reshape(ng, g_rows, HEAD_DIM)
        s_c = jnp.einsum('gqd,gdw->gqw', q_g, side_by_side(ck16, d0, d1), preferred_element_type=_f32)
        s_n = _dot(q_st.reshape(nb * grp_rows, HEAD_DIM), kt16[d0:d1, :]).reshape(ng, g_rows, rows)
        s_c = jnp.where(valid_c, s_c, NEG_BIG)
        s_n = jnp.where(valid_n, s_n, NEG_BIG)
        sink = jnp.concatenate([sink_rows[kv]] * gb, axis=0)[None]
        m = jnp.maximum(jnp.maximum(jnp.max(s_c, axis=-1, keepdims=True),
                                    jnp.max(s_n, axis=-1, keepdims=True)), sink)
        e_c, e_n = jnp.exp(s_c - m), jnp.exp(s_n - m)
        denom = (jnp.sum(e_c, axis=-1, keepdims=True) + jnp.sum(e_n, axis=-1, keepdims=True)
                 + jnp.exp(sink - m))
        o = jnp.einsum('gqw,gdw->gqd', e_c.astype(_bf16), side_by_side(cv16, d0, d1),
                       preferred_element_type=_f32)
        o = o + _dot(e_n.reshape(nb * grp_rows, rows).astype(_bf16),
                     v[:, d0:d1].astype(_bf16)).reshape(ng, g_rows, HEAD_DIM)
        o = (o * (1.0 / denom)).reshape(nb, grp_rows, HEAD_DIM)
        heads += [o[:, g * dec_seq:(g + 1) * dec_seq, :] for g in range(GQA_GROUP)]
    yb = jnp.concatenate(heads, axis=2).reshape(rows, B_WIDTH)
    x_new, hn = _merge(x, ya, yb, g_oa, g_ob, w_out, g_ffn)
    y = _ffn(x_new, hn, w_up, w_down)
    x_all[i] = y
    y_ref[...] = y


def _rope_tables(pos):
    half = ROT_DIM // 2
    inv = np.power(np.float32(ROPE_THETA), -2.0 * np.arange(half, dtype=np.float32) / np.float32(ROT_DIM))
    ang = pos.astype(np.float32)[:, None] * inv.astype(np.float32)[None, :]
    cos, sin = np.cos(ang).astype(np.float32), np.sin(ang).astype(np.float32)
    n = pos.shape[0]
    pad1 = np.ones((n, HEAD_DIM - ROT_DIM), np.float32)
    pad0 = np.zeros((n, HEAD_DIM - ROT_DIM), np.float32)
    zero = np.zeros((n, half), np.float32)
    cos_t = np.concatenate([cos, cos, pad1], axis=1)
    sin_a = np.concatenate([-sin, zero, pad0], axis=1)
    sin_b = np.concatenate([zero, sin, pad0], axis=1)
    rep = LANES // HEAD_DIM
    return np.concatenate([np.tile(t, (1, rep)) for t in (cos_t, sin_a, sin_b)], axis=1)


def kernel(x_prompt, x_sample, cache_win_k, cache_win_v, g_mix, w_in, g_sv, g_q, g_k, w_spatial,
           b_spatial, sinks, g_out_a, g_out_b, w_out, g_ffn, w_up, w_down):
    batch, seq, _ = x_prompt.shape
    dec_batch, dec_seq, _ = x_sample.shape
    depth = w_in.shape[0]
    assert seq % PROMPT_TILE == 0 and PROMPT_TILE % CHUNK == 0
    assert dec_batch % SAMPLE_BATCH_TILE == 0 and CHUNK % dec_seq == 0
    s_rows = SAMPLE_BATCH_TILE * dec_seq
    assert s_rows % CHUNK == 0 and SAMPLE_BATCH_TILE % SAMPLE_ATTN_GROUP == 0
    assert WINDOW % (GQA_GROUP * dec_seq) == 0
    n_tiles = seq // PROMPT_TILE
    s_tiles = dec_batch // SAMPLE_BATCH_TILE

    rope_p = jnp.asarray(_rope_tables(np.arange(seq)))
    rope_s = jnp.asarray(np.tile(_rope_tables(seq + np.arange(dec_seq)), (SAMPLE_BATCH_TILE, 1)))
    hid = np.arange(B_WIDTH) // HEAD_DIM
    ones_bd = jnp.asarray(hid[:, None] == hid[None, :], dtype=_bf16)
    tril = np.tril(np.ones((CHUNK, CHUNK), dtype=bool))
    rep = CHUNK // dec_seq
    expand = jnp.asarray(np.tile(np.eye(dec_seq, dtype=np.float32), (rep, 1)))
    same_batch = jnp.asarray(np.kron(np.eye(rep), np.ones((dec_seq, dec_seq))), dtype=_f32)

    row3 = lambda a: a[:, None, :]
    ws = jnp.where(tril, w_spatial, 0.0)
    bias = jnp.repeat(jnp.swapaxes(b_spatial, 1, 2), HEAD_DIM, axis=2)
    ws_s = jnp.einsum('rt,lhts,cs->lhrc', expand, ws[:, :, :dec_seq, :dec_seq], expand,
                      precision=lax.Precision.HIGHEST) * same_batch
    bias_s = jnp.tile(bias[:, :dec_seq], (1, rep, 1))
    sink_rows = jnp.repeat(sinks.reshape(depth, KV_HEADS, GQA_GROUP), dec_seq, axis=2)[..., None]
    big_f32 = (w_in, w_out, w_up, w_down)
    head = lambda wb: (row3(g_mix), wb[0], row3(g_sv), row3(jnp.tile(g_q, (1, N_HEADS))),
                       row3(jnp.tile(g_k, (1, KV_HEADS))))
    tail = lambda wb: (row3(g_out_a), row3(g_out_b), wb[1], row3(g_ffn), wb[2], wb[3])
    head_shapes = [(1, D_MODEL), (D_MODEL, IN_COLS), (1, A_WIDTH), (1, B_WIDTH), (1, KV_COLS)]
    tail_shapes = [(1, A_WIDTH), (1, B_WIDTH), (D_MODEL, D_MODEL), (1, D_MODEL), (D_MODEL, D_FF), (D_FF, D_MODEL)]
    spatial_shapes = [(A_HEADS, CHUNK, CHUNK), (CHUNK, A_WIDTH)]

    def const_spec(shape):
        nd = len(shape)
        return pl.BlockSpec(shape, lambda *_: (0,) * nd, pipeline_mode=pl.Buffered(1))

    def layer_spec(shape, layer_of, buffers=1):
        nd = len(shape)
        return pl.BlockSpec((None,) + tuple(shape), lambda *g: (layer_of(*g),) + (0,) * nd,
                            pipeline_mode=pl.Buffered(buffers))

    def weight_specs(layer_of, prefetch=()):
        spec = lambda sh: layer_spec(sh, layer_of, 2 if sh in prefetch else 1)
        return ([spec(sh) for sh in head_shapes] + [const_spec((B_WIDTH, B_WIDTH))]
                + [spec(sh) for sh in spatial_shapes + tail_shapes])

    cparams = functools.partial(pltpu.CompilerParams, vmem_limit_bytes=VMEM_LIMIT_BYTES)

    n_slabs = batch * n_tiles
    assert all(w.shape[1] % (n_slabs * 16) == 0 for w in big_f32)
    slab_shape = lambda w, n=n_slabs: (None, w.shape[1] // n, w.shape[2])

    def cast_first_layer_kernel(*refs):
        for src, dst in zip(refs[:len(big_f32)], refs[len(big_f32):]):
            dst[...] = src[...].astype(_bf16)

    wb = pl.pallas_call(
        cast_first_layer_kernel,
        grid=(FIRST_CAST_STEPS,),
        in_specs=[pl.BlockSpec(slab_shape(w, FIRST_CAST_STEPS), lambda s: (0, s, 0)) for w in big_f32],
        out_specs=[pl.BlockSpec(slab_shape(w, FIRST_CAST_STEPS), lambda s: (0, s, 0)) for w in big_f32],
        out_shape=[jax.ShapeDtypeStruct(w.shape, _bf16) for w in big_f32],
        compiler_params=cparams(dimension_semantics=("arbitrary",)),
        name="cast_first_layer",
    )(*big_f32)

    n_fixed_in = 3 + len(head_shapes) + 1 + len(spatial_shapes) + len(tail_shapes)
    big_in_pos = {0: 4, 1: 13, 2: 15, 3: 16}

    def prompt_layer(l, xp, wb):
        cast_next = l + 1 < depth
        slab_of = lambda b, i: (l + 1, b * n_tiles + i, 0)
        in_specs = ([pl.BlockSpec(memory_space=pltpu.SMEM),
                     pl.BlockSpec((None, PROMPT_TILE, D_MODEL), lambda b, i: (b, i, 0)),
                     pl.BlockSpec((PROMPT_TILE, 3 * LANES), lambda b, i: (i, 0))]
                    + weight_specs(lambda b, i: l))
        assert len(in_specs) == n_fixed_in
        out_specs = [pl.BlockSpec((None, PROMPT_TILE, D_MODEL), lambda b, i: (b, i, 0)),
                     pl.BlockSpec((None, WINDOW, KV_COLS), lambda b, i: (b, 0, 0)),
                     pl.BlockSpec((None, WINDOW, KV_COLS), lambda b, i: (b, 0, 0))]
        out_shape = [jax.ShapeDtypeStruct((batch, seq, D_MODEL), _f32),
                     jax.ShapeDtypeStruct((batch, WINDOW, KV_COLS), _f32),
                     jax.ShapeDtypeStruct((batch, WINDOW, KV_COLS), _f32)]
        args = (sinks[l], xp, rope_p, *head(wb), ones_bd, ws.astype(_bf16), bias, *tail(wb))
        aliases = {}
        if cast_next:
            in_specs += [pl.BlockSpec(slab_shape(w), slab_of) for w in big_f32]
            out_specs += [pl.BlockSpec(slab_shape(w), slab_of) for w in big_f32]
            out_shape += [jax.ShapeDtypeStruct(w.shape, _bf16) for w in big_f32]
            args += big_f32
            aliases = {big_in_pos[j]: 3 + j for j in range(len(big_f32))}
        outs = pl.pallas_call(
            functools.partial(_prompt_kernel, n_cast=len(big_f32) if cast_next else 0),
            grid=(batch, n_tiles),
            in_specs=in_specs, out_specs=out_specs, out_shape=out_shape,
            input_output_aliases=aliases,
            scratch_shapes=[pltpu.VMEM((2, CHUNK, KV_COLS), _bf16), pltpu.VMEM((2, CHUNK, KV_COLS), _bf16)],
            compiler_params=cparams(dimension_semantics=("arbitrary", "arbitrary")),
            name="prompt_layer",
        )(*args)
        return outs[0], outs[1], outs[2], (tuple(outs[3:]) if cast_next else wb)

    xp = x_prompt
    nk_p, nv_p = [], []
    for l in range(depth):
        xp, kp, vp, wb = prompt_layer(l, xp, wb)
        nk_p.append(kp)
        nv_p.append(vp)

    win_t = lambda c: jnp.transpose(c, (0, 1, 3, 4, 2)).reshape(depth, dec_batch, KV_COLS, WINDOW)
    win = lambda c: jnp.transpose(c.reshape(depth, dec_batch, KV_HEADS, HEAD_DIM, WINDOW), (0, 1, 4, 2, 3))
    layer_tile = lambda l, i: (l, i, 0, 0)
    ys, ks, vs, va_s = pl.pallas_call(
        _sample_kernel,
        grid=(depth, s_tiles),
        in_specs=[pl.BlockSpec((s_rows, D_MODEL), lambda l, i: (jnp.where(l == 0, i, s_tiles - 1), 0)),
                  const_spec((s_rows, 3 * LANES)),
                  pl.BlockSpec((None, SAMPLE_BATCH_TILE, KV_COLS, WINDOW), layer_tile),
                  pl.BlockSpec((None, SAMPLE_BATCH_TILE, KV_COLS, WINDOW), layer_tile),
                  layer_spec((KV_HEADS, GQA_GROUP * dec_seq, 1), lambda l, i: l)]
                 + weight_specs(lambda l, i: l, prefetch=SAMPLE_PREFETCH_SHAPES),
        out_specs=[pl.BlockSpec((s_rows, D_MODEL), lambda l, i: (jnp.where(l == depth - 1, i, 0), 0)),
                   pl.BlockSpec((None, SAMPLE_BATCH_TILE, KV_COLS, WINDOW), layer_tile),
                   pl.BlockSpec((None, SAMPLE_BATCH_TILE, KV_COLS, WINDOW), layer_tile),
                   pl.BlockSpec((None, s_rows, A_WIDTH), lambda l, i: (l, i, 0))],
        out_shape=[jax.ShapeDtypeStruct((dec_batch * dec_seq, D_MODEL), _f32),
                   jax.ShapeDtypeStruct((depth, dec_batch, KV_COLS, WINDOW), _f32),
                   jax.ShapeDtypeStruct((depth, dec_batch, KV_COLS, WINDOW), _f32),
                   jax.ShapeDtypeStruct((depth, dec_batch * dec_seq, A_WIDTH), _f32)],
        scratch_shapes=[pltpu.VMEM((s_tiles, s_rows, D_MODEL), _f32)],
        compiler_params=cparams(dimension_semantics=("arbitrary", "arbitrary")),
        name="sample_layers",
    )(x_sample.reshape(dec_batch * dec_seq, D_MODEL), rope_s,
      win_t(cache_win_k), win_t(cache_win_v), sink_rows,
      *head(wb), ones_bd, ws_s.astype(_bf16), bias_s, *tail(wb))

    kv5 = lambda t, n: t.reshape(depth, n, WINDOW, KV_HEADS, HEAD_DIM)
    return (xp, ys.reshape(dec_batch, dec_seq, D_MODEL),
            kv5(jnp.stack(nk_p), batch), kv5(jnp.stack(nv_p), batch), win(ks), win(vs),
            va_s.reshape(depth, dec_batch, dec_seq, A_HEADS, HEAD_DIM))
```

```python
import functools

import jax
import jax.numpy as jnp
import numpy as np
from jax import lax
from jax.experimental import pallas as pl
from jax.experimental.pallas import tpu as pltpu

D_MODEL = 1024
HEAD_DIM = 64
A_WIDTH = 512
B_WIDTH = 512
A_HEADS = A_WIDTH // HEAD_DIM
N_HEADS = B_WIDTH // HEAD_DIM
KV_HEADS = 2
GQA_GROUP = N_HEADS // KV_HEADS
KV_COLS = KV_HEADS * HEAD_DIM
CHUNK = 128
WINDOW = 128
ROPE_THETA = 500000.0
ROT_DIM = HEAD_DIM // 4
D_FF = 4 * D_MODEL
EPS = 1e-6
IN_COLS = 2 * A_WIDTH + B_WIDTH + 2 * KV_COLS
Q_OFF = 2 * A_WIDTH
K_OFF = Q_OFF + B_WIDTH
V_OFF = K_OFF + KV_COLS
NEG_BIG = -1e30
LOG2_E = float(np.log2(np.e))

LANES = 128
FF_CHUNK = 1024
PROMPT_TILE = 512
SAMPLE_BATCH_TILE = 16
SCORE_LOOKAHEAD = 2
SAMPLE_ATTN_GROUP = 1
SAMPLE_PREFETCH_SHAPES = ((D_MODEL, IN_COLS), (D_MODEL, D_MODEL), (D_MODEL, D_FF))
FIRST_CAST_STEPS = 8
VMEM_LIMIT_BYTES = 56 * 1024 * 1024

_bf16 = jnp.bfloat16
_f32 = jnp.float32


def _dot(a, b):
    return jnp.dot(a, b, preferred_element_type=_f32)


def _rms(x, g):
    return x * lax.rsqrt(jnp.mean(x * x, axis=-1, keepdims=True) + EPS) * g


def _gelu(x):
    return 0.5 * x * (1.0 + lax.erf(x * np.float32(np.sqrt(0.5))))


def _head_rms(z, g_tiled, ones_bd):
    ss = _dot((z * z).astype(_bf16), ones_bd)
    return z * lax.rsqrt(ss * (1.0 / HEAD_DIM) + EPS) * g_tiled


def _rope(x, cos_t, sin_a, sin_b):
    cols = []
    for c in range(x.shape[1] // LANES):
        xc = x[:, c * LANES:(c + 1) * LANES]
        nxt = pltpu.roll(xc, LANES - ROT_DIM // 2, 1)
        prv = pltpu.roll(xc, ROT_DIM // 2, 1)
        cols.append(xc * cos_t + nxt * sin_a + prv * sin_b)
    return cols[0] if len(cols) == 1 else jnp.concatenate(cols, axis=1)


def _in_proj(x, g_mix, w_in):
    return _dot(_rms(x, g_mix[...]).astype(_bf16), w_in[...])


def _split_inputs(z, rope_ref, g_sv, g_q, g_k, ones_bd):
    u = _gelu(z[:, :A_WIDTH])
    va = _rms(_gelu(z[:, A_WIDTH:2 * A_WIDTH]), g_sv[...])
    cos_t = rope_ref[:, 0:LANES]
    sin_a = rope_ref[:, LANES:2 * LANES]
    sin_b = rope_ref[:, 2 * LANES:3 * LANES]
    ones = ones_bd[...]
    q = _rope(_head_rms(z[:, Q_OFF:K_OFF], g_q[...], ones), cos_t, sin_a, sin_b)
    k = _rope(_head_rms(z[:, K_OFF:V_OFF], g_k[...], ones[:KV_COLS, :KV_COLS]), cos_t, sin_a, sin_b)
    v = z[:, V_OFF:]
    return u, va, q, k, v


def _spatial(va_blk, u_blk, ws_ref, bias):
    vb = va_blk.astype(_bf16)
    outs = [_dot(ws_ref[h], vb[:, h * HEAD_DIM:(h + 1) * HEAD_DIM]) for h in range(A_HEADS)]
    return u_blk * (jnp.concatenate(outs, axis=1) + bias)


def _merge(x, ya, yb, g_oa, g_ob, w_out, g_ffn):
    cat = jnp.concatenate([_rms(ya, g_oa[...]), _rms(yb, g_ob[...])], axis=1).astype(_bf16)
    x = x + _dot(cat, w_out[...])
    return x, _rms(x, g_ffn[...]).astype(_bf16)


def _ffn_chunk(acc, hn, w_up, w_down, c, width):
    h = _dot(hn, w_up[:, c * width:(c + 1) * width])
    h = jnp.maximum(h, 0.0)
    return acc + _dot((h * h).astype(_bf16), w_down[c * width:(c + 1) * width, :])


def _ffn(x, hn, w_up, w_down):
    acc = x
    for c in range(D_FF // FF_CHUNK):
        acc = _ffn_chunk(acc, hn, w_up, w_down, c, FF_CHUNK)
    return acc


def _dot_nt(a, b):
    return lax.dot_general(a, b, (((1,), (1,)), ((), ())), preferred_element_type=_f32)


def _swap_halves(t):
    return pltpu.roll(t, HEAD_DIM, 1)


def _lane_lo(shape):
    return lax.broadcasted_iota(jnp.int32, shape, len(shape) - 1) < HEAD_DIM


def _prompt_kernel(sinks_ref, x_ref, rope_ref, g_mix, w_in, g_sv, g_q, g_k, ones_bd, ws_ref,
                   bias_ref, g_oa, g_ob, w_out, g_ffn, w_up, w_down, *rest, n_cast):
    cast_src, rest = rest[:n_cast], rest[n_cast:]
    y_ref, ks_ref, vs_ref = rest[:3]
    cast_dst, (kprev, vprev) = rest[3:3 + n_cast], rest[3 + n_cast:]
    for src, dst in zip(cast_src, cast_dst):
        dst[...] = src[...].astype(_bf16)

    i = pl.program_id(1)
    tile = x_ref.shape[0]
    n_blk = tile // CHUNK

    @pl.when(i == 0)
    def _():
        kprev[...] = jnp.zeros_like(kprev)
        vprev[...] = jnp.zeros_like(vprev)

    x = x_ref[...]
    xn = _rms(x, g_mix[...]).astype(_bf16)
    z_qkv = _dot(xn, w_in[:, Q_OFF:])
    cos_t, sin_a, sin_b = (rope_ref[:, n * LANES:(n + 1) * LANES] for n in range(3))
    ones = ones_bd[...]
    q = _rope(_head_rms(z_qkv[:, :B_WIDTH], g_q[...], ones), cos_t, sin_a, sin_b)
    k = _rope(_head_rms(z_qkv[:, B_WIDTH:B_WIDTH + KV_COLS], g_k[...], ones[:KV_COLS, :KV_COLS]),
              cos_t, sin_a, sin_b)
    v = z_qkv[:, B_WIDTH + KV_COLS:]
    va = _rms(_gelu(_dot(xn, w_in[:, A_WIDTH:Q_OFF])), g_sv[...])
    u = _gelu(_dot(xn, w_in[:, :A_WIDTH]))
    ks_ref[...] = k[tile - WINDOW:, :]
    vs_ref[...] = v[tile - WINDOW:, :]

    bias = bias_ref[...]
    row = lax.broadcasted_iota(jnp.int32, (CHUNK, 2 * CHUNK), 0)
    col = lax.broadcasted_iota(jnp.int32, (CHUNK, 2 * CHUNK), 1)
    lo = _lane_lo((CHUNK, LANES))
    qs = q * (HEAD_DIM ** -0.5 * LOG2_E)

    ya_blocks, yb_blocks = [], []
    for blk in range(n_blk):
        r0, r1 = blk * CHUNK, (blk + 1) * CHUNK
        ya_blocks.append(_spatial(va[r0:r1], u[r0:r1], ws_ref, bias))

        k_blk, v_blk = k[r0:r1], v[r0:r1]
        k_new = (k_blk.astype(_bf16), _swap_halves(k_blk).astype(_bf16))
        v_new = (v_blk.astype(_bf16), _swap_halves(v_blk).astype(_bf16))
        if blk == 0:
            k_old = (kprev[0], kprev[1])
            v_old = (vprev[0], vprev[1])
            first = jnp.where(i == 0, 2 * CHUNK, 0)
        else:
            first = 0
        valid = jnp.where(col < CHUNK, col - row - first, row + CHUNK - col) >= 0
        k_ctx = [jnp.concatenate([k_old[a], k_new[a]], axis=0) for a in range(2)]
        v_ctx = [jnp.concatenate([v_old[a], v_new[a]], axis=0) for a in range(2)]
        k_old, v_old = k_new, v_new

        def arrangement(h):
            return (h // GQA_GROUP + h % 2) % 2

        def scores(h):
            q_pair = qs[r0:r1, (h // 2) * LANES:(h // 2 + 1) * LANES]
            qm = jnp.where(lo if h % 2 == 0 else ~lo, q_pair, 0.0).astype(_bf16)
            return _dot_nt(qm, k_ctx[arrangement(h)])

        outs = []
        ahead = [scores(h) for h in range(SCORE_LOOKAHEAD)]
        for h in range(N_HEADS):
            sc = jnp.where(valid, ahead.pop(0), NEG_BIG)
            if h + SCORE_LOOKAHEAD < N_HEADS:
                ahead.append(scores(h + SCORE_LOOKAHEAD))
            sink = sinks_ref[h] * LOG2_E
            m = jnp.maximum(jnp.max(sc, axis=-1, keepdims=True), sink)
            e = jnp.exp2(sc - m)
            denom = jnp.sum(e, axis=-1, keepdims=True) + jnp.exp2(sink - m)
            o = _dot(e.astype(_bf16), v_ctx[arrangement(h)])
            outs.append(o * (1.0 / denom))
        yb_blocks.append(jnp.concatenate(
            [jnp.where(lo, outs[2 * p], outs[2 * p + 1]) for p in range(N_HEADS // 2)], axis=1))

    kprev[0], kprev[1] = k_old
    vprev[0], vprev[1] = v_old
    ya = jnp.concatenate(ya_blocks, axis=0)
    yb = jnp.concatenate(yb_blocks, axis=0)
    x_new, hn = _merge(x, ya, yb, g_oa, g_ob, w_out, g_ffn)
    y_ref[...] = _ffn(x_new, hn, w_up, w_down)


def _sample_kernel(x_ref, rope_ref, ck_ref, cv_ref, sink_rows, g_mix, w_in, g_sv, g_q, g_k, ones_bd,
                   ws_ref, bias_ref, g_oa, g_ob, w_out, g_ffn, w_up, w_down,
                   y_ref, ko_ref, vo_ref, va_ref, x_all):
    l = pl.program_id(0)
    i = pl.program_id(1)
    rows = x_ref.shape[0]
    nb = ck_ref.shape[0]
    dec_seq = rows // nb
    grp_rows = GQA_GROUP * dec_seq

    @pl.when(l == 0)
    def _():
        x_all[i] = x_ref[...]

    x = x_all[i]
    u, va, q, k, v = _split_inputs(_in_proj(x, g_mix, w_in), rope_ref, g_sv, g_q, g_k, ones_bd)
    va_ref[...] = va

    bias = bias_ref[...]
    ya = jnp.concatenate(
        [_spatial(va[r:r + CHUNK], u[r:r + CHUNK], ws_ref, bias) for r in range(0, rows, CHUNK)], axis=0)

    ck, cv = ck_ref[...], cv_ref[...]
    k_t, v_t = k.T, v.T
    lane = lax.broadcasted_iota(jnp.int32, (KV_COLS, WINDOW), 1)
    for b in range(nb):
        col = b * dec_seq
        grp, off = col // LANES, col % LANES
        for new_t, cache, out_ref in ((k_t, ck, ko_ref), (v_t, cv, vo_ref)):
            fresh = pltpu.roll(new_t[:, grp * LANES:(grp + 1) * LANES], (LANES - off) % LANES, 1)
            merged = jnp.where(lane < dec_seq, fresh, cache[b])
            out_ref[b] = pltpu.roll(merged, WINDOW - dec_seq, 1)

    gb = SAMPLE_ATTN_GROUP
    ng = nb // gb
    g_rows = gb * grp_rows
    ck16, cv16 = ck.astype(_bf16), cv.astype(_bf16)
    kt16 = k_t.astype(_bf16)
    q3 = (q * (HEAD_DIM ** -0.5)).reshape(nb, dec_seq, B_WIDTH)
    r_c = lax.broadcasted_iota(jnp.int32, (g_rows, gb * WINDOW), 0)
    c_c = lax.broadcasted_iota(jnp.int32, (g_rows, gb * WINDOW), 1)
    w_own = c_c - (r_c - r_c % grp_rows) * (WINDOW // grp_rows)
    valid_c = (jnp.minimum(w_own - r_c % dec_seq, WINDOW - 1 - w_own) >= 0)[None]
    g_n = lax.broadcasted_iota(jnp.int32, (ng, g_rows, rows), 0)
    r_n = lax.broadcasted_iota(jnp.int32, (ng, g_rows, rows), 1)
    c_n = lax.broadcasted_iota(jnp.int32, (ng, g_rows, rows), 2)
    t_key = c_n * GQA_GROUP - (g_n * g_rows + r_n - r_n % grp_rows)
    valid_n = jnp.minimum(t_key, (r_n % dec_seq) * GQA_GROUP - t_key) >= 0

    def side_by_side(t16, d0, d1):
        return jnp.stack([jnp.concatenate([t16[g * gb + j, d0:d1, :] for j in range(gb)], axis=1)
                          for g in range(ng)])

    heads = []
    for kv in range(KV_HEADS):
        d0, d1 = kv * HEAD_DIM, (kv + 1) * HEAD_DIM
        q_st = jnp.concatenate(
            [q3[:, :, h * HEAD_DIM:(h + 1) * HEAD_DIM] for h in range(kv * GQA_GROUP, (kv + 1) * GQA_GROUP)],
            axis=1).astype(_bf16)
        q_g = q_st.reshape(ng, g_rows, HEAD_DIM)
        s_c = jnp.einsum('gqd,gdw->gqw', q_g, side_by_side(ck16, d0, d1), preferred_element_type=_f32)
        s_n = _dot(q_st.reshape(nb * grp_rows, HEAD_DIM), kt16[d0:d1, :]).reshape(ng, g_rows, rows)
        s_c = jnp.where(valid_c, s_c, NEG_BIG)
        s_n = jnp.where(valid_n, s_n, NEG_BIG)
        sink = jnp.concatenate([sink_rows[kv]] * gb, axis=0)[None]
        m = jnp.maximum(jnp.maximum(jnp.max(s_c, axis=-1, keepdims=True),
                                    jnp.max(s_n, axis=-1, keepdims=True)), sink)
        e_c, e_n = jnp.exp(s_c - m), jnp.exp(s_n - m)
        denom = (jnp.sum(e_c, axis=-1, keepdims=True) + jnp.sum(e_n, axis=-1, keepdims=True)
                 + jnp.exp(sink - m))
        o = jnp.einsum('gqw,gdw->gqd', e_c.astype(_bf16), side_by_side(cv16, d0, d1),
                       preferred_element_type=_f32)
        o = o + _dot(e_n.reshape(nb * grp_rows, rows).astype(_bf16),
                     v[:, d0:d1].astype(_bf16)).reshape(ng, g_rows, HEAD_DIM)
        o = (o * (1.0 / denom)).reshape(nb, grp_rows, HEAD_DIM)
        heads += [o[:, g * dec_seq:(g + 1) * dec_seq, :] for g in range(GQA_GROUP)]
    yb = jnp.concatenate(heads, axis=2).reshape(rows, B_WIDTH)
    x_new, hn = _merge(x, ya, yb, g_oa, g_ob, w_out, g_ffn)
    y = _ffn(x_new, hn, w_up, w_down)
    x_all[i] = y
    y_ref[...] = y


def _rope_tables(pos):
    half = ROT_DIM // 2
    inv = np.power(np.float32(ROPE_THETA), -2.0 * np.arange(half, dtype=np.float32) / np.float32(ROT_DIM))
    ang = pos.astype(np.float32)[:, None] * inv.astype(np.float32)[None, :]
    cos, sin = np.cos(ang).astype(np.float32), np.sin(ang).astype(np.float32)
    n = pos.shape[0]
    pad1 = np.ones((n, HEAD_DIM - ROT_DIM), np.float32)
    pad0 = np.zeros((n, HEAD_DIM - ROT_DIM), np.float32)
    zero = np.zeros((n, half), np.float32)
    cos_t = np.concatenate([cos, cos, pad1], axis=1)
    sin_a = np.concatenate([-sin, zero, pad0], axis=1)
    sin_b = np.concatenate([zero, sin, pad0], axis=1)
    rep = LANES // HEAD_DIM
    return np.concatenate([np.tile(t, (1, rep)) for t in (cos_t, sin_a, sin_b)], axis=1)


def kernel(x_prompt, x_sample, cache_win_k, cache_win_v, g_mix, w_in, g_sv, g_q, g_k, w_spatial,
           b_spatial, sinks, g_out_a, g_out_b, w_out, g_ffn, w_up, w_down):
    batch, seq, _ = x_prompt.shape
    dec_batch, dec_seq, _ = x_sample.shape
    depth = w_in.shape[0]
    assert seq % PROMPT_TILE == 0 and PROMPT_TILE % CHUNK == 0
    assert dec_batch % SAMPLE_BATCH_TILE == 0 and CHUNK % dec_seq == 0
    s_rows = SAMPLE_BATCH_TILE * dec_seq
    assert s_rows % CHUNK == 0 and SAMPLE_BATCH_TILE % SAMPLE_ATTN_GROUP == 0
    assert WINDOW % (GQA_GROUP * dec_seq) == 0
    n_tiles = seq // PROMPT_TILE
    s_tiles = dec_batch // SAMPLE_BATCH_TILE

    rope_p = jnp.asarray(_rope_tables(np.arange(seq)))
    rope_s = jnp.asarray(np.tile(_rope_tables(seq + np.arange(dec_seq)), (SAMPLE_BATCH_TILE, 1)))
    hid = np.arange(B_WIDTH) // HEAD_DIM
    ones_bd = jnp.asarray(hid[:, None] == hid[None, :], dtype=_bf16)
    tril = np.tril(np.ones((CHUNK, CHUNK), dtype=bool))
    rep = CHUNK // dec_seq
    expand = jnp.asarray(np.tile(np.eye(dec_seq, dtype=np.float32), (rep, 1)))
    same_batch = jnp.asarray(np.kron(np.eye(rep), np.ones((dec_seq, dec_seq))), dtype=_f32)

    row3 = lambda a: a[:, None, :]
    ws = jnp.where(tril, w_spatial, 0.0)
    bias = jnp.repeat(jnp.swapaxes(b_spatial, 1, 2), HEAD_DIM, axis=2)
    ws_s = jnp.einsum('rt,lhts,cs->lhrc', expand, ws[:, :, :dec_seq, :dec_seq], expand,
                      precision=lax.Precision.HIGHEST) * same_batch
    bias_s = jnp.tile(bias[:, :dec_seq], (1, rep, 1))
    sink_rows = jnp.repeat(sinks.reshape(depth, KV_HEADS, GQA_GROUP), dec_seq, axis=2)[..., None]
    big_f32 = (w_in, w_out, w_up, w_down)
    head = lambda wb: (row3(g_mix), wb[0], row3(g_sv), row3(jnp.tile(g_q, (1, N_HEADS))),
                       row3(jnp.tile(g_k, (1, KV_HEADS))))
    tail = lambda wb: (row3(g_out_a), row3(g_out_b), wb[1], row3(g_ffn), wb[2], wb[3])
    head_shapes = [(1, D_MODEL), (D_MODEL, IN_COLS), (1, A_WIDTH), (1, B_WIDTH), (1, KV_COLS)]
    tail_shapes = [(1, A_WIDTH), (1, B_WIDTH), (D_MODEL, D_MODEL), (1, D_MODEL), (D_MODEL, D_FF), (D_FF, D_MODEL)]
    spatial_shapes = [(A_HEADS, CHUNK, CHUNK), (CHUNK, A_WIDTH)]

    def const_spec(shape):
        nd = len(shape)
        return pl.BlockSpec(shape, lambda *_: (0,) * nd, pipeline_mode=pl.Buffered(1))

    def layer_spec(shape, layer_of, buffers=1):
        nd = len(shape)
        return pl.BlockSpec((None,) + tuple(shape), lambda *g: (layer_of(*g),) + (0,) * nd,
                            pipeline_mode=pl.Buffered(buffers))

    def weight_specs(layer_of, prefetch=()):
        spec = lambda sh: layer_spec(sh, layer_of, 2 if sh in prefetch else 1)
        return ([spec(sh) for sh in head_shapes] + [const_spec((B_WIDTH, B_WIDTH))]
                + [spec(sh) for sh in spatial_shapes + tail_shapes])

    cparams = functools.partial(pltpu.CompilerParams, vmem_limit_bytes=VMEM_LIMIT_BYTES)

    n_slabs = batch * n_tiles
    assert all(w.shape[1] % (n_slabs * 16) == 0 for w in big_f32)
    slab_shape = lambda w, n=n_slabs: (None, w.shape[1] // n, w.shape[2])

    def cast_first_layer_kernel(*refs):
        for src, dst in zip(refs[:len(big_f32)], refs[len(big_f32):]):
            dst[...] = src[...].astype(_bf16)

    wb = pl.pallas_call(
        cast_first_layer_kernel,
        grid=(FIRST_CAST_STEPS,),
        in_specs=[pl.BlockSpec(slab_shape(w, FIRST_CAST_STEPS), lambda s: (0, s, 0)) for w in big_f32],
        out_specs=[pl.BlockSpec(slab_shape(w, FIRST_CAST_STEPS), lambda s: (0, s, 0)) for w in big_f32],
        out_shape=[jax.ShapeDtypeStruct(w.shape, _bf16) for w in big_f32],
        compiler_params=cparams(dimension_semantics=("arbitrary",)),
        name="cast_first_layer",
    )(*big_f32)

    n_fixed_in = 3 + len(head_shapes) + 1 + len(spatial_shapes) + len(tail_shapes)
    big_in_pos = {0: 4, 1: 13, 2: 15, 3: 16}

    def prompt_layer(l, xp, wb):
        cast_next = l + 1 < depth
        slab_of = lambda b, i: (l + 1, b * n_tiles + i, 0)
        in_specs = ([pl.BlockSpec(memory_space=pltpu.SMEM),
                     pl.BlockSpec((None, PROMPT_TILE, D_MODEL), lambda b, i: (b, i, 0)),
                     pl.BlockSpec((PROMPT_TILE, 3 * LANES), lambda b, i: (i, 0))]
                    + weight_specs(lambda b, i: l))
        assert len(in_specs) == n_fixed_in
        out_specs = [pl.BlockSpec((None, PROMPT_TILE, D_MODEL), lambda b, i: (b, i, 0)),
                     pl.BlockSpec((None, WINDOW, KV_COLS), lambda b, i: (b, 0, 0)),
                     pl.BlockSpec((None, WINDOW, KV_COLS), lambda b, i: (b, 0, 0))]
        out_shape = [jax.ShapeDtypeStruct((batch, seq, D_MODEL), _f32),
                     jax.ShapeDtypeStruct((batch, WINDOW, KV_COLS), _f32),
                     jax.ShapeDtypeStruct((batch, WINDOW, KV_COLS), _f32)]
        args = (sinks[l], xp, rope_p, *head(wb), ones_bd, ws.astype(_bf16), bias, *tail(wb))
        aliases = {}
        if cast_next:
            in_specs += [pl.BlockSpec(slab_shape(w), slab_of) for w in big_f32]
            out_specs += [pl.BlockSpec(slab_shape(w), slab_of) for w in big_f32]
            out_shape += [jax.ShapeDtypeStruct(w.shape, _bf16) for w in big_f32]
            args += big_f32
            aliases = {big_in_pos[j]: 3 + j for j in range(len(big_f32))}
        outs = pl.pallas_call(
            functools.partial(_prompt_kernel, n_cast=len(big_f32) if cast_next else 0),
            grid=(batch, n_tiles),
            in_specs=in_specs, out_specs=out_specs, out_shape=out_shape,
            input_output_aliases=aliases,
            scratch_shapes=[pltpu.VMEM((2, CHUNK, KV_COLS), _bf16), pltpu.VMEM((2, CHUNK, KV_COLS), _bf16)],
            compiler_params=cparams(dimension_semantics=("arbitrary", "arbitrary")),
            name="prompt_layer",
        )(*args)
        return outs[0], outs[1], outs[2], (tuple(outs[3:]) if cast_next else wb)

    xp = x_prompt
    nk_p, nv_p = [], []
    for l in range(depth):
        xp, kp, vp, wb = prompt_layer(l, xp, wb)
        nk_p.append(kp)
        nv_p.append(vp)

    win_t = lambda c: jnp.transpose(c, (0, 1, 3, 4, 2)).reshape(depth, dec_batch, KV_COLS, WINDOW)
    win = lambda c: jnp.transpose(c.reshape(depth, dec_batch, KV_HEADS, HEAD_DIM, WINDOW), (0, 1, 4, 2, 3))
    layer_tile = lambda l, i: (l, i, 0, 0)
    ys, ks, vs, va_s = pl.pallas_call(
        _sample_kernel,
        grid=(depth, s_tiles),
        in_specs=[pl.BlockSpec((s_rows, D_MODEL), lambda l, i: (jnp.where(l == 0, i, s_tiles - 1), 0)),
                  const_spec((s_rows, 3 * LANES)),
                  pl.BlockSpec((None, SAMPLE_BATCH_TILE, KV_COLS, WINDOW), layer_tile),
                  pl.BlockSpec((None, SAMPLE_BATCH_TILE, KV_COLS, WINDOW), layer_tile),
                  layer_spec((KV_HEADS, GQA_GROUP * dec_seq, 1), lambda l, i: l)]
                 + weight_specs(lambda l, i: l, prefetch=SAMPLE_PREFETCH_SHAPES),
        out_specs=[pl.BlockSpec((s_rows, D_MODEL), lambda l, i: (jnp.where(l == depth - 1, i, 0), 0)),
                   pl.BlockSpec((None, SAMPLE_BATCH_TILE, KV_COLS, WINDOW), layer_tile),
                   pl.BlockSpec((None, SAMPLE_BATCH_TILE, KV_COLS, WINDOW), layer_tile),
                   pl.BlockSpec((None, s_rows, A_WIDTH), lambda l, i: (l, i, 0))],
        out_shape=[jax.ShapeDtypeStruct((dec_batch * dec_seq, D_MODEL), _f32),
                   jax.ShapeDtypeStruct((depth, dec_batch, KV_COLS, WINDOW), _f32),
                   jax.ShapeDtypeStruct((depth, dec_batch, KV_COLS, WINDOW), _f32),
                   jax.ShapeDtypeStruct((depth, dec_batch * dec_seq, A_WIDTH), _f32)],
        scratch_shapes=[pltpu.VMEM((s_tiles, s_rows, D_MODEL), _f32)],
        compiler_params=cparams(dimension_semantics=("arbitrary", "arbitrary")),
        name="sample_layers",
    )(x_sample.reshape(dec_batch * dec_seq, D_MODEL), rope_s,
      win_t(cache_win_k), win_t(cache_win_v), sink_rows,
      *head(wb), ones_bd, ws_s.astype(_bf16), bias_s, *tail(wb))

    kv5 = lambda t, n: t.reshape(depth, n, WINDOW, KV_HEADS, HEAD_DIM)
    return (xp, ys.reshape(dec_batch, dec_seq, D_MODEL),
            kv5(jnp.stack(nk_p), batch), kv5(jnp.stack(nv_p), batch), win(ks), win(vs),
            va_s.reshape(depth, dec_batch, dec_seq, A_HEADS, HEAD_DIM))
```

```python
import functools

import jax
import jax.numpy as jnp
import numpy as np
from jax import lax
from jax.experimental import pallas as pl
from jax.experimental.pallas import tpu as pltpu

D_MODEL = 1024
HEAD_DIM = 64
A_WIDTH = 512
B_WIDTH = 512
A_HEADS = A_WIDTH // HEAD_DIM
N_HEADS = B_WIDTH // HEAD_DIM
KV_HEADS = 2
GQA_GROUP = N_HEADS // KV_HEADS
KV_COLS = KV_HEADS * HEAD_DIM
CHUNK = 128
WINDOW = 128
ROPE_THETA = 500000.0
ROT_DIM = HEAD_DIM // 4
D_FF = 4 * D_MODEL
EPS = 1e-6
IN_COLS = 2 * A_WIDTH + B_WIDTH + 2 * KV_COLS
Q_OFF = 2 * A_WIDTH
K_OFF = Q_OFF + B_WIDTH
V_OFF = K_OFF + KV_COLS
NEG_BIG = -1e30
LOG2_E = float(np.log2(np.e))

LANES = 128
FF_CHUNK = 1024
PROMPT_TILE = 512
SAMPLE_BATCH_TILE = 16
SCORE_LOOKAHEAD = 2
SAMPLE_ATTN_GROUP = 1
SAMPLE_PREFETCH_SHAPES = ((D_MODEL, IN_COLS), (D_MODEL, D_MODEL), (D_MODEL, D_FF))
FIRST_CAST_STEPS = 8
VMEM_LIMIT_BYTES = 56 * 1024 * 1024

_bf16 = jnp.bfloat16
_f32 = jnp.float32


def _dot(a, b):
    return jnp.dot(a, b, preferred_element_type=_f32)


def _rms(x, g, eps=EPS):
    return x * lax.rsqrt(jnp.mean(x * x, axis=-1, keepdims=True) + eps) * g


def _gelu_x2(x):
    return x * (1.0 + lax.erf(x * np.float32(np.sqrt(0.5))))


def _head_rms(z, g_tiled, ones_bd):
    ss = _dot((z * z).astype(_bf16), ones_bd)
    return z * lax.rsqrt(ss * (1.0 / HEAD_DIM) + EPS) * g_tiled


def _rope(x, cos_t, sin_a, sin_b):
    cols = []
    for c in range(x.shape[1] // LANES):
        xc = x[:, c * LANES:(c + 1) * LANES]
        nxt = pltpu.roll(xc, LANES - ROT_DIM // 2, 1)
        prv = pltpu.roll(xc, ROT_DIM // 2, 1)
        cols.append(xc * cos_t + nxt * sin_a + prv * sin_b)
    return cols[0] if len(cols) == 1 else jnp.concatenate(cols, axis=1)


def _in_proj(x, g_mix, w_in):
    return _dot(_rms(x, g_mix[...]).astype(_bf16), w_in[...])


def _split_inputs(z, rope_ref, g_sv, g_q, g_k, ones_bd):
    u = _gelu_x2(z[:, :A_WIDTH])
    va = _rms(_gelu_x2(z[:, A_WIDTH:2 * A_WIDTH]), g_sv[...], 4.0 * EPS)
    cos_t = rope_ref[:, 0:LANES]
    sin_a = rope_ref[:, LANES:2 * LANES]
    sin_b = rope_ref[:, 2 * LANES:3 * LANES]
    ones = ones_bd[...]
    q = _rope(_head_rms(z[:, Q_OFF:K_OFF], g_q[...], ones), cos_t, sin_a, sin_b)
    k = _rope(_head_rms(z[:, K_OFF:V_OFF], g_k[...], ones[:KV_COLS, :KV_COLS]), cos_t, sin_a, sin_b)
    v = z[:, V_OFF:]
    return u, va, q, k, v


def _spatial(va_blk, u_blk, ws_ref, bias):
    vb = va_blk.astype(_bf16)
    outs = [_dot(ws_ref[h], vb[:, h * HEAD_DIM:(h + 1) * HEAD_DIM]) for h in range(A_HEADS)]
    return u_blk * (jnp.concatenate(outs, axis=1) + bias)


def _merge(x, ya, yb, g_oa, g_ob, w_out, g_ffn):
    cat = jnp.concatenate([_rms(ya, g_oa[...]), _rms(yb, g_ob[...])], axis=1).astype(_bf16)
    x = x + _dot(cat, w_out[...])
    return x, _rms(x, g_ffn[...]).astype(_bf16)


def _ffn_chunk(acc, hn, w_up, w_down, c, width):
    h = _dot(hn, w_up[:, c * width:(c + 1) * width])
    h = jnp.maximum(h, 0.0)
    return acc + _dot((h * h).astype(_bf16), w_down[c * width:(c + 1) * width, :])


def _ffn(x, hn, w_up, w_down):
    acc = x
    for c in range(D_FF // FF_CHUNK):
        acc = _ffn_chunk(acc, hn, w_up, w_down, c, FF_CHUNK)
    return acc


def _dot_nt(a, b):
    return lax.dot_general(a, b, (((1,), (1,)), ((), ())), preferred_element_type=_f32)


def _swap_halves(t):
    return pltpu.roll(t, HEAD_DIM, 1)


def _lane_lo(shape):
    return lax.broadcasted_iota(jnp.int32, shape, len(shape) - 1) < HEAD_DIM


def _prompt_kernel(sinks_ref, x_ref, rope_ref, g_mix, w_in, g_sv, g_q, g_k, ones_bd, ws_ref,
                   bias_ref, g_oa, g_ob, w_out, g_ffn, w_up, w_down, *rest, n_cast):
    cast_src, rest = rest[:n_cast], rest[n_cast:]
    y_ref, ks_ref, vs_ref = rest[:3]
    cast_dst, (kprev, vprev) = rest[3:3 + n_cast], rest[3 + n_cast:]
    for src, dst in zip(cast_src, cast_dst):
        dst[...] = src[...].astype(_bf16)

    i = pl.program_id(1)
    tile = x_ref.shape[0]
    n_blk = tile // CHUNK

    @pl.when(i == 0)
    def _():
        kprev[...] = jnp.zeros_like(kprev)
        vprev[...] = jnp.zeros_like(vprev)

    x = x_ref[...]
    xn = _rms(x, g_mix[...]).astype(_bf16)
    z_qkv = _dot(xn, w_in[:, Q_OFF:])
    cos_t, sin_a, sin_b = (rope_ref[:, n * LANES:(n + 1) * LANES] for n in range(3))
    ones = ones_bd[...]
    q = _rope(_head_rms(z_qkv[:, :B_WIDTH], g_q[...], ones), cos_t, sin_a, sin_b)
    k = _rope(_head_rms(z_qkv[:, B_WIDTH:B_WIDTH + KV_COLS], g_k[...], ones[:KV_COLS, :KV_COLS]),
              cos_t, sin_a, sin_b)
    v = z_qkv[:, B_WIDTH + KV_COLS:]
    va = _rms(_gelu_x2(_dot(xn, w_in[:, A_WIDTH:Q_OFF])), g_sv[...], 4.0 * EPS)
    u = _gelu_x2(_dot(xn, w_in[:, :A_WIDTH]))
    ks_ref[...] = k[tile - WINDOW:, :]
    vs_ref[...] = v[tile - WINDOW:, :]

    bias = bias_ref[...]
    row = lax.broadcasted_iota(jnp.int32, (CHUNK, 2 * CHUNK), 0)
    col = lax.broadcasted_iota(jnp.int32, (CHUNK, 2 * CHUNK), 1)
    lo = _lane_lo((CHUNK, LANES))
    qs = q

    ones_kv = jnp.ones((2 * CHUNK, LANES), _bf16)
    ya_blocks, yb_blocks = [], []
    for blk in range(n_blk):
        r0, r1 = blk * CHUNK, (blk + 1) * CHUNK
        ya_blocks.append(_spatial(va[r0:r1], u[r0:r1], ws_ref, bias))

        k_blk, v_blk = k[r0:r1], v[r0:r1]
        k_new = (k_blk.astype(_bf16), _swap_halves(k_blk).astype(_bf16))
        v_new = (v_blk.astype(_bf16), _swap_halves(v_blk).astype(_bf16))
        if blk == 0:
            k_old = (kprev[0], kprev[1])
            v_old = (vprev[0], vprev[1])
            first = jnp.where(i == 0, 2 * CHUNK, 0)
        else:
            first = 0
        valid = jnp.where(col < CHUNK, col - row - first, row + CHUNK - col) >= 0
        k_ctx = [jnp.concatenate([k_old[a], k_new[a]], axis=0) for a in range(2)]
        v_ctx = [jnp.concatenate([jnp.concatenate([v_old[a], v_new[a]], axis=0), ones_kv], axis=1)
                 for a in range(2)]
        k_old, v_old = k_new, v_new

        def arrangement(h):
            return (h // GQA_GROUP + h % 2) % 2

        def scores(h):
            q_pair = qs[r0:r1, (h // 2) * LANES:(h // 2 + 1) * LANES]
            qm = jnp.where(lo if h % 2 == 0 else ~lo, q_pair, 0.0).astype(_bf16)
            return _dot_nt(qm, k_ctx[arrangement(h)])

        outs = []
        ahead = [scores(h) for h in range(SCORE_LOOKAHEAD)]
        for h in range(N_HEADS):
            sc = jnp.where(valid, ahead.pop(0), NEG_BIG)
            if h + SCORE_LOOKAHEAD < N_HEADS:
                ahead.append(scores(h + SCORE_LOOKAHEAD))
            sink = sinks_ref[h] * LOG2_E
            m = jnp.maximum(jnp.max(sc, axis=-1, keepdims=True), sink)
            e = jnp.exp2(sc - m).astype(_bf16)
            o = _dot(e, v_ctx[arrangement(h)])
            denom = o[:, LANES:] + jnp.exp2(sink - m)
            outs.append(o[:, :LANES] * (1.0 / denom))
        yb_blocks.append(jnp.concatenate(
            [jnp.where(lo, outs[2 * p], outs[2 * p + 1]) for p in range(N_HEADS // 2)], axis=1))

    kprev[0], kprev[1] = k_old
    vprev[0], vprev[1] = v_old
    ya = jnp.concatenate(ya_blocks, axis=0)
    yb = jnp.concatenate(yb_blocks, axis=0)
    x_new, hn = _merge(x, ya, yb, g_oa, g_ob, w_out, g_ffn)
    y_ref[...] = _ffn(x_new, hn, w_up, w_down)


def _sample_kernel(x_ref, rope_ref, ck_ref, cv_ref, sink_rows, g_mix, w_in, g_sv, g_q, g_k, ones_bd,
                   ws_ref, bias_ref, g_oa, g_ob, w_out, g_ffn, w_up, w_down,
                   y_ref, ko_ref, vo_ref, va_ref, x_all):
    l = pl.program_id(0)
    i = pl.program_id(1)
    rows = x_ref.shape[0]
    nb = ck_ref.shape[0]
    dec_seq = rows // nb
    grp_rows = GQA_GROUP * dec_seq

    @pl.when(l == 0)
    def _():
        x_all[i] = x_ref[...]

    x = x_all[i]
    u, va, q, k, v = _split_inputs(_in_proj(x, g_mix, w_in), rope_ref, g_sv, g_q, g_k, ones_bd)
    va_ref[...] = va

    bias = bias_ref[...]
    ya = jnp.concatenate(
        [_spatial(va[r:r + CHUNK], u[r:r + CHUNK], ws_ref, bias) for r in range(0, rows, CHUNK)], axis=0)

    ck, cv = ck_ref[...], cv_ref[...]
    k_t, v_t = k.T, v.T
    lane = lax.broadcasted_iota(jnp.int32, (KV_COLS, WINDOW), 1)
    for b in range(nb):
        col = b * dec_seq
        grp, off = col // LANES, col % LANES
        for new_t, cache, out_ref in ((k_t, ck, ko_ref), (v_t, cv, vo_ref)):
            fresh = pltpu.roll(new_t[:, grp * LANES:(grp + 1) * LANES], (LANES - off) % LANES, 1)
            merged = jnp.where(lane < dec_seq, fresh, cache[b])
            out_ref[b] = pltpu.roll(merged, WINDOW - dec_seq, 1)

    gb = SAMPLE_ATTN_GROUP
    ng = nb // gb
    g_rows = gb * grp_rows
    ck16, cv16 = ck.astype(_bf16), cv.astype(_bf16)
    kt16 = k_t.astype(_bf16)
    q3 = q.reshape(nb, dec_seq, B_WIDTH)
    r_c = lax.broadcasted_iota(jnp.int32, (g_rows, gb * WINDOW), 0)
    c_c = lax.broadcasted_iota(jnp.int32, (g_rows, gb * WINDOW), 1)
    w_own = c_c - (r_c - r_c % grp_rows) * (WINDOW // grp_rows)
    valid_c = (jnp.minimum(w_own - r_c % dec_seq, WINDOW - 1 - w_own) >= 0)[None]
    g_n = lax.broadcasted_iota(jnp.int32, (ng, g_rows, rows), 0)
    r_n = lax.broadcasted_iota(jnp.int32, (ng, g_rows, rows), 1)
    c_n = lax.broadcasted_iota(jnp.int32, (ng, g_rows, rows), 2)
    t_key = c_n * GQA_GROUP - (g_n * g_rows + r_n - r_n % grp_rows)
    valid_n = jnp.minimum(t_key, (r_n % dec_seq) * GQA_GROUP - t_key) >= 0

    def side_by_side(t16, d0, d1):
        return jnp.stack([jnp.concatenate([t16[g * gb + j, d0:d1, :] for j in range(gb)], axis=1)
                          for g in range(ng)])

    heads = []
    for kv in range(KV_HEADS):
        d0, d1 = kv * HEAD_DIM, (kv + 1) * HEAD_DIM
        q_st = jnp.concatenate(
            [q3[:, :, h * HEAD_DIM:(h + 1) * HEAD_DIM] for h in range(kv * GQA_GROUP, (kv + 1) * GQA_GROUP)],
            axis=1).astype(_bf16)
        q_g = q_st.reshape(ng, g_rows, HEAD_DIM)
        s_c = jnp.einsum('gqd,gdw->gqw', q_g, side_by_side(ck16, d0, d1), preferred_element_type=_f32)
        s_n = _dot(q_st.reshape(nb * grp_rows, HEAD_DIM), kt16[d0:d1, :]).reshape(ng, g_rows, rows)
        s_c = jnp.where(valid_c, s_c, NEG_BIG)
        s_n = jnp.where(valid_n, s_n, NEG_BIG)
        sink = jnp.concatenate([sink_rows[kv]] * gb, axis=0)[None]
        m = jnp.maximum(jnp.maximum(jnp.max(s_c, axis=-1, keepdims=True),
                                    jnp.max(s_n, axis=-1, keepdims=True)), sink)
        e_c, e_n = jnp.exp2(s_c - m), jnp.exp2(s_n - m)
        denom = (jnp.sum(e_c, axis=-1, keepdims=True) + jnp.sum(e_n, axis=-1, keepdims=True)
                 + jnp.exp2(sink - m))
        o = jnp.einsum('gqw,gdw->gqd', e_c.astype(_bf16), side_by_side(cv16, d0, d1),
                       preferred_element_type=_f32)
        o = o + _dot(e_n.reshape(nb * grp_rows, rows).astype(_bf16),
                     v[:, d0:d1].astype(_bf16)).reshape(ng, g_rows, HEAD_DIM)
        o = (o * (1.0 / denom)).reshape(nb, grp_rows, HEAD_DIM)
        heads += [o[:, g * dec_seq:(g + 1) * dec_seq, :] for g in range(GQA_GROUP)]
    yb = jnp.concatenate(heads, axis=2).reshape(rows, B_WIDTH)
    x_new, hn = _merge(x, ya, yb, g_oa, g_ob, w_out, g_ffn)
    y = _ffn(x_new, hn, w_up, w_down)
    x_all[i] = y
    y_ref[...] = y


def _rope_tables(pos):
    half = ROT_DIM // 2
    inv = np.power(np.float32(ROPE_THETA), -2.0 * np.arange(half, dtype=np.float32) / np.float32(ROT_DIM))
    ang = pos.astype(np.float32)[:, None] * inv.astype(np.float32)[None, :]
    cos, sin = np.cos(ang).astype(np.float32), np.sin(ang).astype(np.float32)
    n = pos.shape[0]
    pad1 = np.ones((n, HEAD_DIM - ROT_DIM), np.float32)
    pad0 = np.zeros((n, HEAD_DIM - ROT_DIM), np.float32)
    zero = np.zeros((n, half), np.float32)
    cos_t = np.concatenate([cos, cos, pad1], axis=1)
    sin_a = np.concatenate([-sin, zero, pad0], axis=1)
    sin_b = np.concatenate([zero, sin, pad0], axis=1)
    rep = LANES // HEAD_DIM
    return np.concatenate([np.tile(t, (1, rep)) for t in (cos_t, sin_a, sin_b)], axis=1)


def kernel(x_prompt, x_sample, cache_win_k, cache_win_v, g_mix, w_in, g_sv, g_q, g_k, w_spatial,
           b_spatial, sinks, g_out_a, g_out_b, w_out, g_ffn, w_up, w_down):
    batch, seq, _ = x_prompt.shape
    dec_batch, dec_seq, _ = x_sample.shape
    depth = w_in.shape[0]
    assert seq % PROMPT_TILE == 0 and PROMPT_TILE % CHUNK == 0
    assert dec_batch % SAMPLE_BATCH_TILE == 0 and CHUNK % dec_seq == 0
    s_rows = SAMPLE_BATCH_TILE * dec_seq
    assert s_rows % CHUNK == 0 and SAMPLE_BATCH_TILE % SAMPLE_ATTN_GROUP == 0
    assert WINDOW % (GQA_GROUP * dec_seq) == 0
    n_tiles = seq // PROMPT_TILE
    s_tiles = dec_batch // SAMPLE_BATCH_TILE

    rope_p = jnp.asarray(_rope_tables(np.arange(seq)))
    rope_s = jnp.asarray(np.tile(_rope_tables(seq + np.arange(dec_seq)), (SAMPLE_BATCH_TILE, 1)))
    hid = np.arange(B_WIDTH) // HEAD_DIM
    ones_bd = jnp.asarray(hid[:, None] == hid[None, :], dtype=_bf16)
    tril = np.tril(np.ones((CHUNK, CHUNK), dtype=bool))
    rep = CHUNK // dec_seq
    expand = jnp.asarray(np.tile(np.eye(dec_seq, dtype=np.float32), (rep, 1)))
    same_batch = jnp.asarray(np.kron(np.eye(rep), np.ones((dec_seq, dec_seq))), dtype=_f32)

    row3 = lambda a: a[:, None, :]
    ws = jnp.where(tril, 0.5 * w_spatial, 0.0)
    bias = jnp.repeat(jnp.swapaxes(0.5 * b_spatial, 1, 2), HEAD_DIM, axis=2)
    ws_s = jnp.einsum('rt,lhts,cs->lhrc', expand, ws[:, :, :dec_seq, :dec_seq], expand,
                      precision=lax.Precision.HIGHEST) * same_batch
    bias_s = jnp.tile(bias[:, :dec_seq], (1, rep, 1))
    sink_rows = jnp.repeat((sinks * LOG2_E).reshape(depth, KV_HEADS, GQA_GROUP), dec_seq, axis=2)[..., None]
    big_f32 = (w_in, w_out, w_up, w_down)
    g_q_scaled = jnp.tile(g_q, (1, N_HEADS)) * (HEAD_DIM ** -0.5 * LOG2_E)
    head = lambda wb: (row3(g_mix), wb[0], row3(g_sv), row3(g_q_scaled),
                       row3(jnp.tile(g_k, (1, KV_HEADS))))
    tail = lambda wb: (row3(g_out_a), row3(g_out_b), wb[1], row3(g_ffn), wb[2], wb[3])
    head_shapes = [(1, D_MODEL), (D_MODEL, IN_COLS), (1, A_WIDTH), (1, B_WIDTH), (1, KV_COLS)]
    tail_shapes = [(1, A_WIDTH), (1, B_WIDTH), (D_MODEL, D_MODEL), (1, D_MODEL), (D_MODEL, D_FF), (D_FF, D_MODEL)]
    spatial_shapes = [(A_HEADS, CHUNK, CHUNK), (CHUNK, A_WIDTH)]

    def const_spec(shape):
        nd = len(shape)
        return pl.BlockSpec(shape, lambda *_: (0,) * nd, pipeline_mode=pl.Buffered(1))

    def layer_spec(shape, layer_of, buffers=1):
        nd = len(shape)
        return pl.BlockSpec((None,) + tuple(shape), lambda *g: (layer_of(*g),) + (0,) * nd,
                            pipeline_mode=pl.Buffered(buffers))

    def weight_specs(layer_of, prefetch=()):
        spec = lambda sh: layer_spec(sh, layer_of, 2 if sh in prefetch else 1)
        return ([spec(sh) for sh in head_shapes] + [const_spec((B_WIDTH, B_WIDTH))]
                + [spec(sh) for sh in spatial_shapes + tail_shapes])

    cparams = functools.partial(pltpu.CompilerParams, vmem_limit_bytes=VMEM_LIMIT_BYTES)

    n_slabs = batch * n_tiles
    assert all(w.shape[1] % (n_slabs * 16) == 0 for w in big_f32)
    slab_shape = lambda w, n=n_slabs: (None, w.shape[1] // n, w.shape[2])

    def cast_first_layer_kernel(*refs):
        for src, dst in zip(refs[:len(big_f32)], refs[len(big_f32):]):
            dst[...] = src[...].astype(_bf16)

    wb = pl.pallas_call(
        cast_first_layer_kernel,
        grid=(FIRST_CAST_STEPS,),
        in_specs=[pl.BlockSpec(slab_shape(w, FIRST_CAST_STEPS), lambda s: (0, s, 0)) for w in big_f32],
        out_specs=[pl.BlockSpec(slab_shape(w, FIRST_CAST_STEPS), lambda s: (0, s, 0)) for w in big_f32],
        out_shape=[jax.ShapeDtypeStruct(w.shape, _bf16) for w in big_f32],
        compiler_params=cparams(dimension_semantics=("arbitrary",)),
        name="cast_first_layer",
    )(*big_f32)

    n_fixed_in = 3 + len(head_shapes) + 1 + len(spatial_shapes) + len(tail_shapes)
    big_in_pos = {0: 4, 1: 13, 2: 15, 3: 16}

    def prompt_layer(l, xp, wb):
        cast_next = l + 1 < depth
        slab_of = lambda b, i: (l + 1, b * n_tiles + i, 0)
        in_specs = ([pl.BlockSpec(memory_space=pltpu.SMEM),
                     pl.BlockSpec((None, PROMPT_TILE, D_MODEL), lambda b, i: (b, i, 0)),
                     pl.BlockSpec((PROMPT_TILE, 3 * LANES), lambda b, i: (i, 0))]
                    + weight_specs(lambda b, i: l))
        assert len(in_specs) == n_fixed_in
        out_specs = [pl.BlockSpec((None, PROMPT_TILE, D_MODEL), lambda b, i: (b, i, 0)),
                     pl.BlockSpec((None, WINDOW, KV_COLS), lambda b, i: (b, 0, 0)),
                     pl.BlockSpec((None, WINDOW, KV_COLS), lambda b, i: (b, 0, 0))]
        out_shape = [jax.ShapeDtypeStruct((batch, seq, D_MODEL), _f32),
                     jax.ShapeDtypeStruct((batch, WINDOW, KV_COLS), _f32),
                     jax.ShapeDtypeStruct((batch, WINDOW, KV_COLS), _f32)]
        args = (sinks[l], xp, rope_p, *head(wb), ones_bd, ws.astype(_bf16), bias, *tail(wb))
        aliases = {}
        if cast_next:
            in_specs += [pl.BlockSpec(slab_shape(w), slab_of) for w in big_f32]
            out_specs += [pl.BlockSpec(slab_shape(w), slab_of) for w in big_f32]
            out_shape += [jax.ShapeDtypeStruct(w.shape, _bf16) for w in big_f32]
            args += big_f32
            aliases = {big_in_pos[j]: 3 + j for j in range(len(big_f32))}
        outs = pl.pallas_call(
            functools.partial(_prompt_kernel, n_cast=len(big_f32) if cast_next else 0),
            grid=(batch, n_tiles),
            in_specs=in_specs, out_specs=out_specs, out_shape=out_shape,
            input_output_aliases=aliases,
            scratch_shapes=[pltpu.VMEM((2, CHUNK, KV_COLS), _bf16), pltpu.VMEM((2, CHUNK, KV_COLS), _bf16)],
            compiler_params=cparams(dimension_semantics=("arbitrary", "arbitrary")),
            name="prompt_layer",
        )(*args)
        return outs[0], outs[1], outs[2], (tuple(outs[3:]) if cast_next else wb)

    xp = x_prompt
    nk_p, nv_p = [], []
    for l in range(depth):
        xp, kp, vp, wb = prompt_layer(l, xp, wb)
        nk_p.append(kp)
        nv_p.append(vp)

    win_t = lambda c: jnp.transpose(c, (0, 1, 3, 4, 2)).reshape(depth, dec_batch, KV_COLS, WINDOW)
    win = lambda c: jnp.transpose(c.reshape(depth, dec_batch, KV_HEADS, HEAD_DIM, WINDOW), (0, 1, 4, 2, 3))
    layer_tile = lambda l, i: (l, i, 0, 0)
    ys, ks, vs, va_s = pl.pallas_call(
        _sample_kernel,
        grid=(depth, s_tiles),
        in_specs=[pl.BlockSpec((s_rows, D_MODEL), lambda l, i: (jnp.where(l == 0, i, s_tiles - 1), 0)),
                  const_spec((s_rows, 3 * LANES)),
                  pl.BlockSpec((None, SAMPLE_BATCH_TILE, KV_COLS, WINDOW), layer_tile),
                  pl.BlockSpec((None, SAMPLE_BATCH_TILE, KV_COLS, WINDOW), layer_tile),
                  layer_spec((KV_HEADS, GQA_GROUP * dec_seq, 1), lambda l, i: l)]
                 + weight_specs(lambda l, i: l, prefetch=SAMPLE_PREFETCH_SHAPES),
        out_specs=[pl.BlockSpec((s_rows, D_MODEL), lambda l, i: (jnp.where(l == depth - 1, i, 0), 0)),
                   pl.BlockSpec((None, SAMPLE_BATCH_TILE, KV_COLS, WINDOW), layer_tile),
                   pl.BlockSpec((None, SAMPLE_BATCH_TILE, KV_COLS, WINDOW), layer_tile),
                   pl.BlockSpec((None, s_rows, A_WIDTH), lambda l, i: (l, i, 0))],
        out_shape=[jax.ShapeDtypeStruct((dec_batch * dec_seq, D_MODEL), _f32),
                   jax.ShapeDtypeStruct((depth, dec_batch, KV_COLS, WINDOW), _f32),
                   jax.ShapeDtypeStruct((depth, dec_batch, KV_COLS, WINDOW), _f32),
                   jax.ShapeDtypeStruct((depth, dec_batch * dec_seq, A_WIDTH), _f32)],
        scratch_shapes=[pltpu.VMEM((s_tiles, s_rows, D_MODEL), _f32)],
        compiler_params=cparams(dimension_semantics=("arbitrary", "arbitrary")),
        name="sample_layers",
    )(x_sample.reshape(dec_batch * dec_seq, D_MODEL), rope_s,
      win_t(cache_win_k), win_t(cache_win_v), sink_rows,
      *head(wb), ones_bd, ws_s.astype(_bf16), bias_s, *tail(wb))

    kv5 = lambda t, n: t.reshape(depth, n, WINDOW, KV_HEADS, HEAD_DIM)
    return (xp, ys.reshape(dec_batch, dec_seq, D_MODEL),
            kv5(jnp.stack(nk_p), batch), kv5(jnp.stack(nv_p), batch), win(ks), win(vs),
            va_s.reshape(depth, dec_batch, dec_seq, A_HEADS, HEAD_DIM))
```

```python
import functools

import jax
import jax.numpy as jnp
import numpy as np
from jax import lax
from jax.experimental import pallas as pl
from jax.experimental.pallas import tpu as pltpu

D_MODEL = 1024
HEAD_DIM = 64
A_WIDTH = 512
B_WIDTH = 512
A_HEADS = A_WIDTH // HEAD_DIM
N_HEADS = B_WIDTH // HEAD_DIM
KV_HEADS = 2
GQA_GROUP = N_HEADS // KV_HEADS
KV_COLS = KV_HEADS * HEAD_DIM
CHUNK = 128
WINDOW = 128
ROPE_THETA = 500000.0
ROT_DIM = HEAD_DIM // 4
D_FF = 4 * D_MODEL
EPS = 1e-6
IN_COLS = 2 * A_WIDTH + B_WIDTH + 2 * KV_COLS
Q_OFF = 2 * A_WIDTH
K_OFF = Q_OFF + B_WIDTH
V_OFF = K_OFF + KV_COLS
NEG_BIG = -1e30
LOG2_E = float(np.log2(np.e))

LANES = 128
FF_CHUNK = 1024
PROMPT_TILE = 512
SAMPLE_BATCH_TILE = 16
PV_LAG = 1
SCORE_LOOKAHEAD = 2
SAMPLE_PREFETCH_SHAPES = ((D_MODEL, IN_COLS), (D_MODEL, D_MODEL), (D_MODEL, D_FF))
FIRST_CAST_STEPS = 8
VMEM_LIMIT_BYTES = 56 * 1024 * 1024

_bf16 = jnp.bfloat16
_f32 = jnp.float32


def _dot(a, b):
    return jnp.dot(a, b, preferred_element_type=_f32)


def _rms(x, g, eps=EPS):
    return x * lax.rsqrt(jnp.mean(x * x, axis=-1, keepdims=True) + eps) * g


def _gelu_x2(x):
    return x * (1.0 + lax.erf(x * np.float32(np.sqrt(0.5))))


def _head_rms(z, g_tiled, ones_bd, eps=EPS):
    ss = _dot((z * z).astype(_bf16), ones_bd)
    return z * lax.rsqrt(ss * (1.0 / HEAD_DIM) + eps) * g_tiled


def _rope(x, cos_t, sin_a, sin_b):
    cols = []
    for c in range(x.shape[1] // LANES):
        xc = x[:, c * LANES:(c + 1) * LANES]
        nxt = pltpu.roll(xc, LANES - ROT_DIM // 2, 1)
        prv = pltpu.roll(xc, ROT_DIM // 2, 1)
        cols.append(xc * cos_t + nxt * sin_a + prv * sin_b)
    return cols[0] if len(cols) == 1 else jnp.concatenate(cols, axis=1)


def _inv_rms(x):
    return lax.rsqrt(jnp.mean(x * x, axis=-1, keepdims=True) + EPS)


def _in_proj(x, w_in):
    return _dot((x * _inv_rms(x)).astype(_bf16), w_in[...])


def _split_inputs(z, rope_ref, g_sv, g_q, g_k, ones_bd):
    u = _gelu_x2(z[:, :A_WIDTH])
    va = _rms(_gelu_x2(z[:, A_WIDTH:2 * A_WIDTH]), g_sv[...], 4.0 * EPS)
    cos_t = rope_ref[:, 0:LANES]
    sin_a = rope_ref[:, LANES:2 * LANES]
    sin_b = rope_ref[:, 2 * LANES:3 * LANES]
    ones = ones_bd[...]
    q = _rope(_head_rms(z[:, Q_OFF:K_OFF], g_q[...], ones), cos_t, sin_a, sin_b)
    k = _rope(_head_rms(z[:, K_OFF:V_OFF], g_k[...], ones[:KV_COLS, :KV_COLS]), cos_t, sin_a, sin_b)
    v = z[:, V_OFF:]
    return u, va, q, k, v


def _spatial(va_blk, u_blk, ws_ref, bias):
    vb = va_blk.astype(_bf16)
    outs = [_dot(ws_ref[h], vb[:, h * HEAD_DIM:(h + 1) * HEAD_DIM]) for h in range(A_HEADS)]
    return u_blk * (jnp.concatenate(outs, axis=1) + bias)


def _merge(x, ya, yb, g_oa, g_ob, w_out):
    cat = jnp.concatenate([_rms(ya, g_oa[...]), _rms(yb, g_ob[...])], axis=1).astype(_bf16)
    return x + _dot(cat, w_out[...])


def _ffn(x, w_up, w_down):
    x16 = x.astype(_bf16)
    acc = None
    for c in range(D_FF // FF_CHUNK):
        h = jnp.maximum(_dot(x16, w_up[:, c * FF_CHUNK:(c + 1) * FF_CHUNK]), 0.0)
        part = _dot((h * h).astype(_bf16), w_down[c * FF_CHUNK:(c + 1) * FF_CHUNK, :])
        acc = part if acc is None else acc + part
    r = _inv_rms(x)
    return x + acc * (r * r)


def _cast_slabs(srcs, gains, dsts):
    for src, gain, dst in zip(srcs, gains, dsts):
        dst[...] = (src[...] * gain[...]).astype(_bf16)


def _dot_nt(a, b):
    return lax.dot_general(a, b, (((1,), (1,)), ((), ())), preferred_element_type=_f32)


def _swap_halves(t):
    return pltpu.roll(t, HEAD_DIM, 1)


def _lane_lo(shape):
    return lax.broadcasted_iota(jnp.int32, shape, len(shape) - 1) < HEAD_DIM


def _prompt_kernel(sinks_ref, x_ref, rope_ref, w_in, g_sv, g_q, g_k, ones_bd, ws_ref,
                   bias_ref, g_oa, g_ob, w_out, w_up, w_down, *rest, n_cast):
    cast_src, cast_gain, rest = rest[:n_cast], rest[n_cast:2 * n_cast], rest[2 * n_cast:]
    y_ref, ks_ref, vs_ref = rest[:3]
    cast_dst, (kprev, vprev) = rest[3:3 + n_cast], rest[3 + n_cast:]
    _cast_slabs(cast_src, cast_gain, cast_dst)

    i = pl.program_id(1)
    tile = x_ref.shape[0]
    n_blk = tile // CHUNK

    @pl.when(i == 0)
    def _():
        kprev[...] = jnp.zeros_like(kprev)
        vprev[...] = jnp.zeros_like(vprev)

    x = x_ref[...]
    xn = x.astype(_bf16)
    mean_sq = jnp.mean(x * x, axis=-1, keepdims=True) + EPS
    r_mix = lax.rsqrt(mean_sq)
    z_qkv = _dot(xn, w_in[:, Q_OFF:])
    cos_t, sin_a, sin_b = (rope_ref[:, n * LANES:(n + 1) * LANES] for n in range(3))
    ones = ones_bd[...]
    head_eps = EPS * mean_sq
    q = _rope(_head_rms(z_qkv[:, :B_WIDTH], g_q[...], ones, head_eps), cos_t, sin_a, sin_b)
    k = _rope(_head_rms(z_qkv[:, B_WIDTH:B_WIDTH + KV_COLS], g_k[...], ones[:KV_COLS, :KV_COLS], head_eps),
              cos_t, sin_a, sin_b)
    v = z_qkv[:, B_WIDTH + KV_COLS:] * r_mix
    even_lanes = lax.broadcasted_iota(jnp.int32, q.shape, 1) % LANES < HEAD_DIM
    q_heads = (jnp.where(even_lanes, q, 0.0).astype(_bf16), jnp.where(even_lanes, 0.0, q).astype(_bf16))
    va = _rms(_gelu_x2(_dot(xn, w_in[:, A_WIDTH:Q_OFF]) * r_mix), g_sv[...], 4.0 * EPS)
    u = _gelu_x2(_dot(xn, w_in[:, :A_WIDTH]) * r_mix)
    ks_ref[...] = k[tile - WINDOW:, :]
    vs_ref[...] = v[tile - WINDOW:, :]

    bias = bias_ref[...]
    row = lax.broadcasted_iota(jnp.int32, (CHUNK, 2 * CHUNK), 0)
    col = lax.broadcasted_iota(jnp.int32, (CHUNK, 2 * CHUNK), 1)
    lo = _lane_lo((CHUNK, LANES))

    ones_kv = jnp.ones((2 * CHUNK, LANES), _bf16)
    ya_blocks, yb_blocks = [], []
    for blk in range(n_blk):
        r0, r1 = blk * CHUNK, (blk + 1) * CHUNK
        ya_blocks.append(_spatial(va[r0:r1], u[r0:r1], ws_ref, bias))

        k_blk, v_blk = k[r0:r1], v[r0:r1]
        k_new = (k_blk.astype(_bf16), _swap_halves(k_blk).astype(_bf16))
        v_new = (v_blk.astype(_bf16), _swap_halves(v_blk).astype(_bf16))
        if blk == 0:
            k_old = (kprev[0], kprev[1])
            v_old = (vprev[0], vprev[1])
            first = jnp.where(i == 0, 2 * CHUNK, 0)
        else:
            first = 0
        valid = jnp.where(col < CHUNK, col - row - first, row + CHUNK - col) >= 0
        k_ctx = [jnp.concatenate([k_old[a], k_new[a]], axis=0) for a in range(2)]
        v_ctx = [jnp.concatenate([jnp.concatenate([v_old[a], v_new[a]], axis=0), ones_kv], axis=1)
                 for a in range(2)]
        k_old, v_old = k_new, v_new

        def arrangement(h):
            return (h // GQA_GROUP + h % 2) % 2

        def scores(h):
            qm = q_heads[h % 2][r0:r1, (h // 2) * LANES:(h // 2 + 1) * LANES]
            return _dot_nt(qm, k_ctx[arrangement(h)])

        outs = []
        ahead = [scores(h) for h in range(SCORE_LOOKAHEAD)]
        behind = []

        def finish(hh, e, sink_term):
            o = _dot(e, v_ctx[arrangement(hh)])
            outs.append(o[:, :LANES] * (1.0 / (o[:, LANES:] + sink_term)))

        for h in range(N_HEADS):
            sc = jnp.where(valid, ahead.pop(0), NEG_BIG)
            if h + SCORE_LOOKAHEAD < N_HEADS:
                ahead.append(scores(h + SCORE_LOOKAHEAD))
            sink = sinks_ref[h] * LOG2_E
            m = jnp.maximum(jnp.max(sc, axis=-1, keepdims=True), sink)
            behind.append((h, jnp.exp2(sc - m).astype(_bf16), jnp.exp2(sink - m)))
            if len(behind) > PV_LAG:
                finish(*behind.pop(0))
        while behind:
            finish(*behind.pop(0))
        yb_blocks.append(jnp.concatenate(
            [jnp.where(lo, outs[2 * p], outs[2 * p + 1]) for p in range(N_HEADS // 2)], axis=1))

    kprev[0], kprev[1] = k_old
    vprev[0], vprev[1] = v_old
    ya = jnp.concatenate(ya_blocks, axis=0)
    yb = jnp.concatenate(yb_blocks, axis=0)
    y_ref[...] = _ffn(_merge(x, ya, yb, g_oa, g_ob, w_out), w_up, w_down)


def _sample_kernel(x_ref, rope_ref, ck_ref, cv_ref, sink_rows, w_in, g_sv, g_q, g_k, ones_bd,
                   ws_ref, bias_ref, g_oa, g_ob, w_out, w_up, w_down,
                   y_ref, ko_ref, vo_ref, va_ref, x_all):
    l = pl.program_id(0)
    i = pl.program_id(1)
    rows = x_ref.shape[0]
    nb = ck_ref.shape[0]
    dec_seq = rows // nb
    grp_rows = GQA_GROUP * dec_seq

    @pl.when(l == 0)
    def _():
        x_all[i] = x_ref[...]

    x = x_all[i]
    u, va, q, k, v = _split_inputs(_in_proj(x, w_in), rope_ref, g_sv, g_q, g_k, ones_bd)
    va_ref[...] = va

    bias = bias_ref[...]
    ya = jnp.concatenate(
        [_spatial(va[r:r + CHUNK], u[r:r + CHUNK], ws_ref, bias) for r in range(0, rows, CHUNK)], axis=0)

    ck, cv = ck_ref[...], cv_ref[...]
    k_t, v_t = k.T, v.T
    lane = lax.broadcasted_iota(jnp.int32, (KV_COLS, WINDOW), 1)
    for b in range(nb):
        col = b * dec_seq
        grp, off = col // LANES, col % LANES
        for new_t, cache, out_ref in ((k_t, ck, ko_ref), (v_t, cv, vo_ref)):
            fresh = pltpu.roll(new_t[:, grp * LANES:(grp + 1) * LANES], (LANES - off) % LANES, 1)
            merged = jnp.where(lane < dec_seq, fresh, cache[b])
            out_ref[b] = pltpu.roll(merged, WINDOW - dec_seq, 1)

    ck16, cv16 = ck.astype(_bf16), cv.astype(_bf16)
    kt16 = k_t.astype(_bf16)
    q3 = q.reshape(nb, dec_seq, B_WIDTH)
    t_c = lax.broadcasted_iota(jnp.int32, (grp_rows, WINDOW), 0) % dec_seq
    w_c = lax.broadcasted_iota(jnp.int32, (grp_rows, WINDOW), 1)
    valid_c = (w_c >= t_c)[None]
    b_n = lax.broadcasted_iota(jnp.int32, (nb, grp_rows, rows), 0)
    t_n = lax.broadcasted_iota(jnp.int32, (nb, grp_rows, rows), 1) % dec_seq
    c_n = lax.broadcasted_iota(jnp.int32, (nb, grp_rows, rows), 2)
    t_key = c_n - b_n * dec_seq
    valid_n = jnp.minimum(t_key, t_n - t_key) >= 0

    heads = []
    for kv in range(KV_HEADS):
        d0, d1 = kv * HEAD_DIM, (kv + 1) * HEAD_DIM
        q_st = jnp.concatenate(
            [q3[:, :, h * HEAD_DIM:(h + 1) * HEAD_DIM] for h in range(kv * GQA_GROUP, (kv + 1) * GQA_GROUP)],
            axis=1).astype(_bf16)
        s_c = jnp.einsum('bqd,bdw->bqw', q_st, ck16[:, d0:d1, :], preferred_element_type=_f32)
        s_n = _dot(q_st.reshape(nb * grp_rows, HEAD_DIM), kt16[d0:d1, :]).reshape(nb, grp_rows, rows)
        s_c = jnp.where(valid_c, s_c, NEG_BIG)
        s_n = jnp.where(valid_n, s_n, NEG_BIG)
        sink = sink_rows[kv][None]
        m = jnp.maximum(jnp.maximum(jnp.max(s_c, axis=-1, keepdims=True),
                                    jnp.max(s_n, axis=-1, keepdims=True)), sink)
        e_c, e_n = jnp.exp2(s_c - m), jnp.exp2(s_n - m)
        denom = (jnp.sum(e_c, axis=-1, keepdims=True) + jnp.sum(e_n, axis=-1, keepdims=True)
                 + jnp.exp2(sink - m))
        o = jnp.einsum('bqw,bdw->bqd', e_c.astype(_bf16), cv16[:, d0:d1, :], preferred_element_type=_f32)
        o = o + _dot(e_n.reshape(nb * grp_rows, rows).astype(_bf16),
                     v[:, d0:d1].astype(_bf16)).reshape(nb, grp_rows, HEAD_DIM)
        o = o * (1.0 / denom)
        heads += [o[:, g * dec_seq:(g + 1) * dec_seq, :] for g in range(GQA_GROUP)]
    yb = jnp.concatenate(heads, axis=2).reshape(rows, B_WIDTH)
    y = _ffn(_merge(x, ya, yb, g_oa, g_ob, w_out), w_up, w_down)
    x_all[i] = y
    y_ref[...] = y


def _rope_tables(pos):
    half = ROT_DIM // 2
    inv = np.power(np.float32(ROPE_THETA), -2.0 * np.arange(half, dtype=np.float32) / np.float32(ROT_DIM))
    ang = pos.astype(np.float32)[:, None] * inv.astype(np.float32)[None, :]
    cos, sin = np.cos(ang).astype(np.float32), np.sin(ang).astype(np.float32)
    n = pos.shape[0]
    pad1 = np.ones((n, HEAD_DIM - ROT_DIM), np.float32)
    pad0 = np.zeros((n, HEAD_DIM - ROT_DIM), np.float32)
    zero = np.zeros((n, half), np.float32)
    cos_t = np.concatenate([cos, cos, pad1], axis=1)
    sin_a = np.concatenate([-sin, zero, pad0], axis=1)
    sin_b = np.concatenate([zero, sin, pad0], axis=1)
    rep = LANES // HEAD_DIM
    return np.concatenate([np.tile(t, (1, rep)) for t in (cos_t, sin_a, sin_b)], axis=1)


def kernel(x_prompt, x_sample, cache_win_k, cache_win_v, g_mix, w_in, g_sv, g_q, g_k, w_spatial,
           b_spatial, sinks, g_out_a, g_out_b, w_out, g_ffn, w_up, w_down):
    batch, seq, _ = x_prompt.shape
    dec_batch, dec_seq, _ = x_sample.shape
    depth = w_in.shape[0]
    assert seq % PROMPT_TILE == 0 and PROMPT_TILE % CHUNK == 0
    assert dec_batch % SAMPLE_BATCH_TILE == 0 and CHUNK % dec_seq == 0
    s_rows = SAMPLE_BATCH_TILE * dec_seq
    assert s_rows % CHUNK == 0
    n_tiles = seq // PROMPT_TILE
    s_tiles = dec_batch // SAMPLE_BATCH_TILE

    rope_p = jnp.asarray(_rope_tables(np.arange(seq)))
    rope_s = jnp.asarray(np.tile(_rope_tables(seq + np.arange(dec_seq)), (SAMPLE_BATCH_TILE, 1)))
    hid = np.arange(B_WIDTH) // HEAD_DIM
    ones_bd = jnp.asarray(hid[:, None] == hid[None, :], dtype=_bf16)
    tril = np.tril(np.ones((CHUNK, CHUNK), dtype=bool))
    rep = CHUNK // dec_seq
    expand = jnp.asarray(np.tile(np.eye(dec_seq, dtype=np.float32), (rep, 1)))
    same_batch = jnp.asarray(np.kron(np.eye(rep), np.ones((dec_seq, dec_seq))), dtype=_f32)

    row3 = lambda a: a[:, None, :]
    ws = jnp.where(tril, 0.5 * w_spatial, 0.0)
    bias = jnp.repeat(jnp.swapaxes(0.5 * b_spatial, 1, 2), HEAD_DIM, axis=2)
    ws_s = jnp.einsum('rt,lhts,cs->lhrc', expand, ws[:, :, :dec_seq, :dec_seq], expand,
                      precision=lax.Precision.HIGHEST) * same_batch
    bias_s = jnp.tile(bias[:, :dec_seq], (1, rep, 1))
    sink_rows = jnp.repeat((sinks * LOG2_E).reshape(depth, KV_HEADS, GQA_GROUP), dec_seq, axis=2)[..., None]
    big_f32 = (w_in, w_out, w_up, w_down)
    col3 = lambda a: a[:, :, None]
    big_gain = (col3(g_mix), jnp.ones((depth, D_MODEL, 1), _f32), col3(g_ffn), jnp.ones((depth, D_FF, 1), _f32))
    g_q_scaled = jnp.tile(g_q, (1, N_HEADS)) * (HEAD_DIM ** -0.5 * LOG2_E)
    head = lambda wb: (wb[0], row3(g_sv), row3(g_q_scaled), row3(jnp.tile(g_k, (1, KV_HEADS))))
    tail = lambda wb: (row3(g_out_a), row3(g_out_b), wb[1], wb[2], wb[3])
    head_shapes = [(D_MODEL, IN_COLS), (1, A_WIDTH), (1, B_WIDTH), (1, KV_COLS)]
    tail_shapes = [(1, A_WIDTH), (1, B_WIDTH), (D_MODEL, D_MODEL), (D_MODEL, D_FF), (D_FF, D_MODEL)]
    spatial_shapes = [(A_HEADS, CHUNK, CHUNK), (CHUNK, A_WIDTH)]

    def const_spec(shape):
        nd = len(shape)
        return pl.BlockSpec(shape, lambda *_: (0,) * nd, pipeline_mode=pl.Buffered(1))

    def layer_spec(shape, layer_of, buffers=1):
        nd = len(shape)
        return pl.BlockSpec((None,) + tuple(shape), lambda *g: (layer_of(*g),) + (0,) * nd,
                            pipeline_mode=pl.Buffered(buffers))

    def weight_specs(layer_of, prefetch=()):
        spec = lambda sh: layer_spec(sh, layer_of, 2 if sh in prefetch else 1)
        return ([spec(sh) for sh in head_shapes] + [const_spec((B_WIDTH, B_WIDTH))]
                + [spec(sh) for sh in spatial_shapes + tail_shapes])

    cparams = functools.partial(pltpu.CompilerParams, vmem_limit_bytes=VMEM_LIMIT_BYTES)

    n_slabs = batch * n_tiles
    assert all(w.shape[1] % (n_slabs * 16) == 0 for w in big_f32)
    slab_shape = lambda w, n=n_slabs: (None, w.shape[1] // n, w.shape[2])
    n_big = len(big_f32)

    def cast_first_layer_kernel(*refs):
        _cast_slabs(refs[:n_big], refs[n_big:2 * n_big], refs[2 * n_big:])

    wb = pl.pallas_call(
        cast_first_layer_kernel,
        grid=(FIRST_CAST_STEPS,),
        in_specs=[pl.BlockSpec(slab_shape(w, FIRST_CAST_STEPS), lambda s: (0, s, 0)) for w in big_f32 + big_gain],
        out_specs=[pl.BlockSpec(slab_shape(w, FIRST_CAST_STEPS), lambda s: (0, s, 0)) for w in big_f32],
        out_shape=[jax.ShapeDtypeStruct(w.shape, _bf16) for w in big_f32],
        compiler_params=cparams(dimension_semantics=("arbitrary",)),
        name="cast_first_layer",
    )(*big_f32, *big_gain)

    n_fixed_in = 3 + len(head_shapes) + 1 + len(spatial_shapes) + len(tail_shapes)
    big_in_pos = {0: 3, 1: 12, 2: 13, 3: 14}

    def prompt_layer(l, xp, wb):
        cast_next = l + 1 < depth
        slab_of = lambda b, i: (l + 1, b * n_tiles + i, 0)
        in_specs = ([pl.BlockSpec(memory_space=pltpu.SMEM),
                     pl.BlockSpec((None, PROMPT_TILE, D_MODEL), lambda b, i: (b, i, 0)),
                     pl.BlockSpec((PROMPT_TILE, 3 * LANES), lambda b, i: (i, 0))]
                    + weight_specs(lambda b, i: l))
        assert len(in_specs) == n_fixed_in
        out_specs = [pl.BlockSpec((None, PROMPT_TILE, D_MODEL), lambda b, i: (b, i, 0)),
                     pl.BlockSpec((None, WINDOW, KV_COLS), lambda b, i: (b, 0, 0)),
                     pl.BlockSpec((None, WINDOW, KV_COLS), lambda b, i: (b, 0, 0))]
        out_shape = [jax.ShapeDtypeStruct((batch, seq, D_MODEL), _f32),
                     jax.ShapeDtypeStruct((batch, WINDOW, KV_COLS), _f32),
                     jax.ShapeDtypeStruct((batch, WINDOW, KV_COLS), _f32)]
        args = (sinks[l], xp, rope_p, *head(wb), ones_bd, ws.astype(_bf16), bias, *tail(wb))
        aliases = {}
        if cast_next:
            in_specs += [pl.BlockSpec(slab_shape(w), slab_of) for w in big_f32 + big_gain]
            out_specs += [pl.BlockSpec(slab_shape(w), slab_of) for w in big_f32]
            out_shape += [jax.ShapeDtypeStruct(w.shape, _bf16) for w in big_f32]
            args += big_f32 + big_gain
            aliases = {big_in_pos[j]: 3 + j for j in range(n_big)}
        outs = pl.pallas_call(
            functools.partial(_prompt_kernel, n_cast=n_big if cast_next else 0),
            grid=(batch, n_tiles),
            in_specs=in_specs, out_specs=out_specs, out_shape=out_shape,
            input_output_aliases=aliases,
            scratch_shapes=[pltpu.VMEM((2, CHUNK, KV_COLS), _bf16), pltpu.VMEM((2, CHUNK, KV_COLS), _bf16)],
            compiler_params=cparams(dimension_semantics=("arbitrary", "arbitrary")),
            name="prompt_layer",
        )(*args)
        return outs[0], outs[1], outs[2], (tuple(outs[3:]) if cast_next else wb)

    xp = x_prompt
    nk_p, nv_p = [], []
    for l in range(depth):
        xp, kp, vp, wb = prompt_layer(l, xp, wb)
        nk_p.append(kp)
        nv_p.append(vp)

    win_t = lambda c: jnp.transpose(c, (0, 1, 3, 4, 2)).reshape(depth, dec_batch, KV_COLS, WINDOW)
    win = lambda c: jnp.transpose(c.reshape(depth, dec_batch, KV_HEADS, HEAD_DIM, WINDOW), (0, 1, 4, 2, 3))
    layer_tile = lambda l, i: (l, i, 0, 0)
    ys, ks, vs, va_s = pl.pallas_call(
        _sample_kernel,
        grid=(depth, s_tiles),
        in_specs=[pl.BlockSpec((s_rows, D_MODEL), lambda l, i: (jnp.where(l == 0, i, s_tiles - 1), 0)),
                  const_spec((s_rows, 3 * LANES)),
                  pl.BlockSpec((None, SAMPLE_BATCH_TILE, KV_COLS, WINDOW), layer_tile),
                  pl.BlockSpec((None, SAMPLE_BATCH_TILE, KV_COLS, WINDOW), layer_tile),
                  layer_spec((KV_HEADS, GQA_GROUP * dec_seq, 1), lambda l, i: l)]
                 + weight_specs(lambda l, i: l, prefetch=SAMPLE_PREFETCH_SHAPES),
        out_specs=[pl.BlockSpec((s_rows, D_MODEL), lambda l, i: (jnp.where(l == depth - 1, i, 0), 0)),
                   pl.BlockSpec((None, SAMPLE_BATCH_TILE, KV_COLS, WINDOW), layer_tile),
                   pl.BlockSpec((None, SAMPLE_BATCH_TILE, KV_COLS, WINDOW), layer_tile),
                   pl.BlockSpec((None, s_rows, A_WIDTH), lambda l, i: (l, i, 0))],
        out_shape=[jax.ShapeDtypeStruct((dec_batch * dec_seq, D_MODEL), _f32),
                   jax.ShapeDtypeStruct((depth, dec_batch, KV_COLS, WINDOW), _f32),
                   jax.ShapeDtypeStruct((depth, dec_batch, KV_COLS, WINDOW), _f32),
                   jax.ShapeDtypeStruct((depth, dec_batch * dec_seq, A_WIDTH), _f32)],
        scratch_shapes=[pltpu.VMEM((s_tiles, s_rows, D_MODEL), _f32)],
        compiler_params=cparams(dimension_semantics=("arbitrary", "arbitrary")),
        name="sample_layers",
    )(x_sample.reshape(dec_batch * dec_seq, D_MODEL), rope_s,
      win_t(cache_win_k), win_t(cache_win_v), sink_rows,
      *head(wb), ones_bd, ws_s.astype(_bf16), bias_s, *tail(wb))

    kv5 = lambda t, n: t.reshape(depth, n, WINDOW, KV_HEADS, HEAD_DIM)
    return (xp, ys.reshape(dec_batch, dec_seq, D_MODEL),
            kv5(jnp.stack(nk_p), batch), kv5(jnp.stack(nv_p), batch), win(ks), win(vs),
            va_s.reshape(depth, dec_batch, dec_seq, A_HEADS, HEAD_DIM))
```

```python
import functools

import jax
import jax.numpy as jnp
import numpy as np
from jax import lax
from jax.experimental import pallas as pl
from jax.experimental.pallas import tpu as pltpu

D_MODEL = 1024
HEAD_DIM = 64
A_WIDTH = 512
B_WIDTH = 512
A_HEADS = A_WIDTH // HEAD_DIM
N_HEADS = B_WIDTH // HEAD_DIM
KV_HEADS = 2
GQA_GROUP = N_HEADS // KV_HEADS
KV_COLS = KV_HEADS * HEAD_DIM
CHUNK = 128
WINDOW = 128
ROPE_THETA = 500000.0
ROT_DIM = HEAD_DIM // 4
D_FF = 4 * D_MODEL
EPS = 1e-6
IN_COLS = 2 * A_WIDTH + B_WIDTH + 2 * KV_COLS
Q_OFF = 2 * A_WIDTH
K_OFF = Q_OFF + B_WIDTH
V_OFF = K_OFF + KV_COLS
NEG_BIG = -1e30
LOG2_E = float(np.log2(np.e))

LANES = 128
FF_CHUNK = 1024
PROMPT_TILE = 512
SAMPLE_BATCH_TILE = 16
PV_LAG = 1
SCORE_LOOKAHEAD = 2
SAMPLE_PREFETCH_SHAPES = ((D_MODEL, IN_COLS), (D_MODEL, D_MODEL), (D_MODEL, D_FF))
FIRST_CAST_STEPS = 8
VMEM_LIMIT_BYTES = 56 * 1024 * 1024

_bf16 = jnp.bfloat16
_f32 = jnp.float32


def _dot(a, b):
    return jnp.dot(a, b, preferred_element_type=_f32)


def _rms(x, g, eps=EPS):
    return x * lax.rsqrt(jnp.mean(x * x, axis=-1, keepdims=True) + eps) * g


def _gelu_x2(x):
    return x * (1.0 + lax.erf(x * np.float32(np.sqrt(0.5))))


def _head_rms(z, g_tiled, ones_bd, eps=EPS):
    ss = _dot((z * z).astype(_bf16), ones_bd)
    return z * lax.rsqrt(ss * (1.0 / HEAD_DIM) + eps) * g_tiled


def _rope(x, cos_t, sin_a, sin_b):
    cols = []
    for c in range(x.shape[1] // LANES):
        xc = x[:, c * LANES:(c + 1) * LANES]
        nxt = pltpu.roll(xc, LANES - ROT_DIM // 2, 1)
        prv = pltpu.roll(xc, ROT_DIM // 2, 1)
        cols.append(xc * cos_t + nxt * sin_a + prv * sin_b)
    return cols[0] if len(cols) == 1 else jnp.concatenate(cols, axis=1)


def _inv_rms(x):
    return lax.rsqrt(jnp.mean(x * x, axis=-1, keepdims=True) + EPS)


def _mixer_inputs_small(x, rope_ref, w_in, g_sv, g_q, g_k, ones_bd):
    z = _dot((x * _inv_rms(x)).astype(_bf16), w_in[...])
    u = _gelu_x2(z[:, :A_WIDTH])
    va = _rms(_gelu_x2(z[:, A_WIDTH:2 * A_WIDTH]), g_sv[...], 4.0 * EPS)
    cos_t, sin_a, sin_b = (rope_ref[:, n * LANES:(n + 1) * LANES] for n in range(3))
    ones = ones_bd[...]
    q = _rope(_head_rms(z[:, Q_OFF:K_OFF], g_q[...], ones), cos_t, sin_a, sin_b)
    k = _rope(_head_rms(z[:, K_OFF:V_OFF], g_k[...], ones[:KV_COLS, :KV_COLS]), cos_t, sin_a, sin_b)
    return u, va, q, k, z[:, V_OFF:]


def _mixer_inputs(x, rope_ref, w_in, g_sv, g_q, g_k, ones_bd):
    xn = x.astype(_bf16)
    mean_sq = jnp.mean(x * x, axis=-1, keepdims=True) + EPS
    r_mix = lax.rsqrt(mean_sq)
    z_qkv = _dot(xn, w_in[:, Q_OFF:])
    cos_t, sin_a, sin_b = (rope_ref[:, n * LANES:(n + 1) * LANES] for n in range(3))
    ones = ones_bd[...]
    head_eps = EPS * mean_sq
    q = _rope(_head_rms(z_qkv[:, :B_WIDTH], g_q[...], ones, head_eps), cos_t, sin_a, sin_b)
    k = _rope(_head_rms(z_qkv[:, B_WIDTH:B_WIDTH + KV_COLS], g_k[...], ones[:KV_COLS, :KV_COLS], head_eps),
              cos_t, sin_a, sin_b)
    v = z_qkv[:, B_WIDTH + KV_COLS:] * r_mix
    va = _rms(_gelu_x2(_dot(xn, w_in[:, A_WIDTH:Q_OFF]) * r_mix), g_sv[...], 4.0 * EPS)
    u = _gelu_x2(_dot(xn, w_in[:, :A_WIDTH]) * r_mix)
    return u, va, q, k, v


def _spatial(va_blk, u_blk, ws_ref, bias):
    vb = va_blk.astype(_bf16)
    outs = [_dot(ws_ref[h], vb[:, h * HEAD_DIM:(h + 1) * HEAD_DIM]) for h in range(A_HEADS)]
    return u_blk * (jnp.concatenate(outs, axis=1) + bias)


def _merge(x, ya, yb, g_oa, g_ob, w_out):
    cat = jnp.concatenate([_rms(ya, g_oa[...]), _rms(yb, g_ob[...])], axis=1).astype(_bf16)
    return x + _dot(cat, w_out[...])


def _ffn(x, w_up, w_down):
    x16 = x.astype(_bf16)
    acc = None
    for c in range(D_FF // FF_CHUNK):
        h = jnp.maximum(_dot(x16, w_up[:, c * FF_CHUNK:(c + 1) * FF_CHUNK]), 0.0)
        part = _dot((h * h).astype(_bf16), w_down[c * FF_CHUNK:(c + 1) * FF_CHUNK, :])
        acc = part if acc is None else acc + part
    r = _inv_rms(x)
    return x + acc * (r * r)


def _cast_slabs(srcs, gains, dsts):
    for src, gain, dst in zip(srcs, gains, dsts):
        dst[...] = (src[...] * gain[...]).astype(_bf16)


def _dot_nt(a, b):
    return lax.dot_general(a, b, (((1,), (1,)), ((), ())), preferred_element_type=_f32)


def _swap_halves(t):
    return pltpu.roll(t, HEAD_DIM, 1)


def _lane_lo(shape):
    return lax.broadcasted_iota(jnp.int32, shape, len(shape) - 1) < HEAD_DIM


def _prompt_kernel(sinks_ref, x_ref, rope_ref, w_in, g_sv, g_q, g_k, ones_bd, ws_ref,
                   bias_ref, g_oa, g_ob, w_out, w_up, w_down, *rest, n_cast):
    cast_src, cast_gain, rest = rest[:n_cast], rest[n_cast:2 * n_cast], rest[2 * n_cast:]
    y_ref, ks_ref, vs_ref = rest[:3]
    cast_dst, (kprev, vprev) = rest[3:3 + n_cast], rest[3 + n_cast:]
    _cast_slabs(cast_src, cast_gain, cast_dst)

    i = pl.program_id(1)
    tile = x_ref.shape[0]
    n_blk = tile // CHUNK

    @pl.when(i == 0)
    def _():
        kprev[...] = jnp.zeros_like(kprev)
        vprev[...] = jnp.zeros_like(vprev)

    x = x_ref[...]
    u, va, q, k, v = _mixer_inputs(x, rope_ref, w_in, g_sv, g_q, g_k, ones_bd)
    even_lanes = lax.broadcasted_iota(jnp.int32, q.shape, 1) % LANES < HEAD_DIM
    q_heads = (jnp.where(even_lanes, q, 0.0).astype(_bf16), jnp.where(even_lanes, 0.0, q).astype(_bf16))
    ks_ref[...] = k[tile - WINDOW:, :]
    vs_ref[...] = v[tile - WINDOW:, :]

    bias = bias_ref[...]
    row = lax.broadcasted_iota(jnp.int32, (CHUNK, 2 * CHUNK), 0)
    col = lax.broadcasted_iota(jnp.int32, (CHUNK, 2 * CHUNK), 1)
    lo = _lane_lo((CHUNK, LANES))

    ones_kv = jnp.ones((2 * CHUNK, LANES), _bf16)
    ya_blocks, yb_blocks = [], []
    for blk in range(n_blk):
        r0, r1 = blk * CHUNK, (blk + 1) * CHUNK
        ya_blocks.append(_spatial(va[r0:r1], u[r0:r1], ws_ref, bias))

        k_blk, v_blk = k[r0:r1], v[r0:r1]
        k_new = (k_blk.astype(_bf16), _swap_halves(k_blk).astype(_bf16))
        v_new = (v_blk.astype(_bf16), _swap_halves(v_blk).astype(_bf16))
        if blk == 0:
            k_old = (kprev[0], kprev[1])
            v_old = (vprev[0], vprev[1])
            first = jnp.where(i == 0, 2 * CHUNK, 0)
        else:
            first = 0
        valid = jnp.where(col < CHUNK, col - row - first, row + CHUNK - col) >= 0
        k_ctx = [jnp.concatenate([k_old[a], k_new[a]], axis=0) for a in range(2)]
        v_ctx = [jnp.concatenate([jnp.concatenate([v_old[a], v_new[a]], axis=0), ones_kv], axis=1)
                 for a in range(2)]
        k_old, v_old = k_new, v_new

        def arrangement(h):
            return (h // GQA_GROUP + h % 2) % 2

        def scores(h):
            qm = q_heads[h % 2][r0:r1, (h // 2) * LANES:(h // 2 + 1) * LANES]
            return _dot_nt(qm, k_ctx[arrangement(h)])

        outs = []
        ahead = [scores(h) for h in range(SCORE_LOOKAHEAD)]
        behind = []

        def finish(hh, e, sink_term):
            o = _dot(e, v_ctx[arrangement(hh)])
            outs.append(o[:, :LANES] * (1.0 / (o[:, LANES:] + sink_term)))

        for h in range(N_HEADS):
            sc = jnp.where(valid, ahead.pop(0), NEG_BIG)
            if h + SCORE_LOOKAHEAD < N_HEADS:
                ahead.append(scores(h + SCORE_LOOKAHEAD))
            sink = sinks_ref[h] * LOG2_E
            m = jnp.maximum(jnp.max(sc, axis=-1, keepdims=True), sink)
            behind.append((h, jnp.exp2(sc - m).astype(_bf16), jnp.exp2(sink - m)))
            if len(behind) > PV_LAG:
                finish(*behind.pop(0))
        while behind:
            finish(*behind.pop(0))
        yb_blocks.append(jnp.concatenate(
            [jnp.where(lo, outs[2 * p], outs[2 * p + 1]) for p in range(N_HEADS // 2)], axis=1))

    kprev[0], kprev[1] = k_old
    vprev[0], vprev[1] = v_old
    ya = jnp.concatenate(ya_blocks, axis=0)
    yb = jnp.concatenate(yb_blocks, axis=0)
    y_ref[...] = _ffn(_merge(x, ya, yb, g_oa, g_ob, w_out), w_up, w_down)


def _sample_kernel(x_ref, rope_ref, ck_ref, cv_ref, sink_rows, w_in, g_sv, g_q, g_k, ones_bd,
                   ws_ref, bias_ref, g_oa, g_ob, w_out, w_up, w_down,
                   y_ref, ko_ref, vo_ref, va_ref, x_all):
    l = pl.program_id(0)
    i = pl.program_id(1)
    rows = x_ref.shape[0]
    nb = ck_ref.shape[0]
    dec_seq = rows // nb
    grp_rows = GQA_GROUP * dec_seq

    @pl.when(l == 0)
    def _():
        x_all[i] = x_ref[...]

    x = x_all[i]
    u, va, q, k, v = _mixer_inputs_small(x, rope_ref, w_in, g_sv, g_q, g_k, ones_bd)
    va_ref[...] = va

    bias = bias_ref[...]
    ya = jnp.concatenate(
        [_spatial(va[r:r + CHUNK], u[r:r + CHUNK], ws_ref, bias) for r in range(0, rows, CHUNK)], axis=0)

    ck, cv = ck_ref[...], cv_ref[...]
    k_t, v_t = k.T, v.T
    lane = lax.broadcasted_iota(jnp.int32, (KV_COLS, WINDOW), 1)
    for b in range(nb):
        col = b * dec_seq
        grp, off = col // LANES, col % LANES
        for new_t, cache, out_ref in ((k_t, ck, ko_ref), (v_t, cv, vo_ref)):
            fresh = pltpu.roll(new_t[:, grp * LANES:(grp + 1) * LANES], (LANES - off) % LANES, 1)
            merged = jnp.where(lane < dec_seq, fresh, cache[b])
            out_ref[b] = pltpu.roll(merged, WINDOW - dec_seq, 1)

    ck16, cv16 = ck.astype(_bf16), cv.astype(_bf16)
    kt16 = k_t.astype(_bf16)
    q3 = q.reshape(nb, dec_seq, B_WIDTH)
    t_c = lax.broadcasted_iota(jnp.int32, (grp_rows, WINDOW), 0) % dec_seq
    w_c = lax.broadcasted_iota(jnp.int32, (grp_rows, WINDOW), 1)
    valid_c = (w_c >= t_c)[None]
    b_n = lax.broadcasted_iota(jnp.int32, (nb, grp_rows, rows), 0)
    t_n = lax.broadcasted_iota(jnp.int32, (nb, grp_rows, rows), 1) % dec_seq
    c_n = lax.broadcasted_iota(jnp.int32, (nb, grp_rows, rows), 2)
    t_key = c_n - b_n * dec_seq
    valid_n = jnp.minimum(t_key, t_n - t_key) >= 0

    heads = []
    for kv in range(KV_HEADS):
        d0, d1 = kv * HEAD_DIM, (kv + 1) * HEAD_DIM
        q_st = jnp.concatenate(
            [q3[:, :, h * HEAD_DIM:(h + 1) * HEAD_DIM] for h in range(kv * GQA_GROUP, (kv + 1) * GQA_GROUP)],
            axis=1).astype(_bf16)
        s_c = jnp.einsum('bqd,bdw->bqw', q_st, ck16[:, d0:d1, :], preferred_element_type=_f32)
        s_n = _dot(q_st.reshape(nb * grp_rows, HEAD_DIM), kt16[d0:d1, :]).reshape(nb, grp_rows, rows)
        s_c = jnp.where(valid_c, s_c, NEG_BIG)
        s_n = jnp.where(valid_n, s_n, NEG_BIG)
        sink = sink_rows[kv][None]
        m = jnp.maximum(jnp.maximum(jnp.max(s_c, axis=-1, keepdims=True),
                                    jnp.max(s_n, axis=-1, keepdims=True)), sink)
        e_c, e_n = jnp.exp2(s_c - m), jnp.exp2(s_n - m)
        denom = (jnp.sum(e_c, axis=-1, keepdims=True) + jnp.sum(e_n, axis=-1, keepdims=True)
                 + jnp.exp2(sink - m))
        o = jnp.einsum('bqw,bdw->bqd', e_c.astype(_bf16), cv16[:, d0:d1, :], preferred_element_type=_f32)
        o = o + _dot(e_n.reshape(nb * grp_rows, rows).astype(_bf16),
                     v[:, d0:d1].astype(_bf16)).reshape(nb, grp_rows, HEAD_DIM)
        o = o * (1.0 / denom)
        heads += [o[:, g * dec_seq:(g + 1) * dec_seq, :] for g in range(GQA_GROUP)]
    yb = jnp.concatenate(heads, axis=2).reshape(rows, B_WIDTH)
    y = _ffn(_merge(x, ya, yb, g_oa, g_ob, w_out), w_up, w_down)
    x_all[i] = y
    y_ref[...] = y


def _rope_tables(pos):
    half = ROT_DIM // 2
    inv = np.power(np.float32(ROPE_THETA), -2.0 * np.arange(half, dtype=np.float32) / np.float32(ROT_DIM))
    ang = pos.astype(np.float32)[:, None] * inv.astype(np.float32)[None, :]
    cos, sin = np.cos(ang).astype(np.float32), np.sin(ang).astype(np.float32)
    n = pos.shape[0]
    pad1 = np.ones((n, HEAD_DIM - ROT_DIM), np.float32)
    pad0 = np.zeros((n, HEAD_DIM - ROT_DIM), np.float32)
    zero = np.zeros((n, half), np.float32)
    cos_t = np.concatenate([cos, cos, pad1], axis=1)
    sin_a = np.concatenate([-sin, zero, pad0], axis=1)
    sin_b = np.concatenate([zero, sin, pad0], axis=1)
    rep = LANES // HEAD_DIM
    return np.concatenate([np.tile(t, (1, rep)) for t in (cos_t, sin_a, sin_b)], axis=1)


def kernel(x_prompt, x_sample, cache_win_k, cache_win_v, g_mix, w_in, g_sv, g_q, g_k, w_spatial,
           b_spatial, sinks, g_out_a, g_out_b, w_out, g_ffn, w_up, w_down):
    batch, seq, _ = x_prompt.shape
    dec_batch, dec_seq, _ = x_sample.shape
    depth = w_in.shape[0]
    assert seq % PROMPT_TILE == 0 and PROMPT_TILE % CHUNK == 0
    assert dec_batch % SAMPLE_BATCH_TILE == 0 and CHUNK % dec_seq == 0
    s_rows = SAMPLE_BATCH_TILE * dec_seq
    assert s_rows % CHUNK == 0
    n_tiles = seq // PROMPT_TILE
    s_tiles = dec_batch // SAMPLE_BATCH_TILE

    rope_p = jnp.asarray(_rope_tables(np.arange(seq)))
    rope_s = jnp.asarray(np.tile(_rope_tables(seq + np.arange(dec_seq)), (SAMPLE_BATCH_TILE, 1)))
    hid = np.arange(B_WIDTH) // HEAD_DIM
    ones_bd = jnp.asarray(hid[:, None] == hid[None, :], dtype=_bf16)
    tril = np.tril(np.ones((CHUNK, CHUNK), dtype=bool))
    rep = CHUNK // dec_seq
    expand = jnp.asarray(np.tile(np.eye(dec_seq, dtype=np.float32), (rep, 1)))
    same_batch = jnp.asarray(np.kron(np.eye(rep), np.ones((dec_seq, dec_seq))), dtype=_f32)

    row3 = lambda a: a[:, None, :]
    ws = jnp.where(tril, 0.5 * w_spatial, 0.0)
    bias = jnp.repeat(jnp.swapaxes(0.5 * b_spatial, 1, 2), HEAD_DIM, axis=2)
    ws_s = jnp.einsum('rt,lhts,cs->lhrc', expand, ws[:, :, :dec_seq, :dec_seq], expand,
                      precision=lax.Precision.HIGHEST) * same_batch
    bias_s = jnp.tile(bias[:, :dec_seq], (1, rep, 1))
    sink_rows = jnp.repeat((sinks * LOG2_E).reshape(depth, KV_HEADS, GQA_GROUP), dec_seq, axis=2)[..., None]
    big_f32 = (w_in, w_out, w_up, w_down)
    col3 = lambda a: a[:, :, None]
    big_gain = (col3(g_mix), jnp.ones((depth, D_MODEL, 1), _f32), col3(g_ffn), jnp.ones((depth, D_FF, 1), _f32))
    g_q_scaled = jnp.tile(g_q, (1, N_HEADS)) * (HEAD_DIM ** -0.5 * LOG2_E)
    head = lambda wb: (wb[0], row3(g_sv), row3(g_q_scaled), row3(jnp.tile(g_k, (1, KV_HEADS))))
    tail = lambda wb: (row3(g_out_a), row3(g_out_b), wb[1], wb[2], wb[3])
    head_shapes = [(D_MODEL, IN_COLS), (1, A_WIDTH), (1, B_WIDTH), (1, KV_COLS)]
    tail_shapes = [(1, A_WIDTH), (1, B_WIDTH), (D_MODEL, D_MODEL), (D_MODEL, D_FF), (D_FF, D_MODEL)]
    spatial_shapes = [(A_HEADS, CHUNK, CHUNK), (CHUNK, A_WIDTH)]

    def const_spec(shape):
        nd = len(shape)
        return pl.BlockSpec(shape, lambda *_: (0,) * nd, pipeline_mode=pl.Buffered(1))

    def layer_spec(shape, layer_of, buffers=1):
        nd = len(shape)
        return pl.BlockSpec((None,) + tuple(shape), lambda *g: (layer_of(*g),) + (0,) * nd,
                            pipeline_mode=pl.Buffered(buffers))

    def weight_specs(layer_of, prefetch=()):
        spec = lambda sh: layer_spec(sh, layer_of, 2 if sh in prefetch else 1)
        return ([spec(sh) for sh in head_shapes] + [const_spec((B_WIDTH, B_WIDTH))]
                + [spec(sh) for sh in spatial_shapes + tail_shapes])

    cparams = functools.partial(pltpu.CompilerParams, vmem_limit_bytes=VMEM_LIMIT_BYTES)

    n_slabs = batch * n_tiles
    assert all(w.shape[1] % (n_slabs * 16) == 0 for w in big_f32)
    slab_shape = lambda w, n=n_slabs: (None, w.shape[1] // n, w.shape[2])
    n_big = len(big_f32)

    def cast_first_layer_kernel(*refs):
        _cast_slabs(refs[:n_big], refs[n_big:2 * n_big], refs[2 * n_big:])

    wb = pl.pallas_call(
        cast_first_layer_kernel,
        grid=(FIRST_CAST_STEPS,),
        in_specs=[pl.BlockSpec(slab_shape(w, FIRST_CAST_STEPS), lambda s: (0, s, 0)) for w in big_f32 + big_gain],
        out_specs=[pl.BlockSpec(slab_shape(w, FIRST_CAST_STEPS), lambda s: (0, s, 0)) for w in big_f32],
        out_shape=[jax.ShapeDtypeStruct(w.shape, _bf16) for w in big_f32],
        compiler_params=cparams(dimension_semantics=("arbitrary",)),
        name="cast_first_layer",
    )(*big_f32, *big_gain)

    n_fixed_in = 3 + len(head_shapes) + 1 + len(spatial_shapes) + len(tail_shapes)
    big_in_pos = {0: 3, 1: 12, 2: 13, 3: 14}

    def prompt_layer(l, xp, wb):
        cast_next = l + 1 < depth
        slab_of = lambda b, i: (l + 1, b * n_tiles + i, 0)
        in_specs = ([pl.BlockSpec(memory_space=pltpu.SMEM),
                     pl.BlockSpec((None, PROMPT_TILE, D_MODEL), lambda b, i: (b, i, 0)),
                     pl.BlockSpec((PROMPT_TILE, 3 * LANES), lambda b, i: (i, 0))]
                    + weight_specs(lambda b, i: l))
        assert len(in_specs) == n_fixed_in
        out_specs = [pl.BlockSpec((None, PROMPT_TILE, D_MODEL), lambda b, i: (b, i, 0)),
                     pl.BlockSpec((None, WINDOW, KV_COLS), lambda b, i: (b, 0, 0)),
                     pl.BlockSpec((None, WINDOW, KV_COLS), lambda b, i: (b, 0, 0))]
        out_shape = [jax.ShapeDtypeStruct((batch, seq, D_MODEL), _f32),
                     jax.ShapeDtypeStruct((batch, WINDOW, KV_COLS), _f32),
                     jax.ShapeDtypeStruct((batch, WINDOW, KV_COLS), _f32)]
        args = (sinks[l], xp, rope_p, *head(wb), ones_bd, ws.astype(_bf16), bias, *tail(wb))
        aliases = {}
        if cast_next:
            in_specs += [pl.BlockSpec(slab_shape(w), slab_of) for w in big_f32 + big_gain]
            out_specs += [pl.BlockSpec(slab_shape(w), slab_of) for w in big_f32]
            out_shape += [jax.ShapeDtypeStruct(w.shape, _bf16) for w in big_f32]
            args += big_f32 + big_gain
            aliases = {big_in_pos[j]: 3 + j for j in range(n_big)}
        outs = pl.pallas_call(
            functools.partial(_prompt_kernel, n_cast=n_big if cast_next else 0),
            grid=(batch, n_tiles),
            in_specs=in_specs, out_specs=out_specs, out_shape=out_shape,
            input_output_aliases=aliases,
            scratch_shapes=[pltpu.VMEM((2, CHUNK, KV_COLS), _bf16), pltpu.VMEM((2, CHUNK, KV_COLS), _bf16)],
            compiler_params=cparams(dimension_semantics=("arbitrary", "arbitrary")),
            name="prompt_layer",
        )(*args)
        return outs[0], outs[1], outs[2], (tuple(outs[3:]) if cast_next else wb)

    xp = x_prompt
    nk_p, nv_p = [], []
    for l in range(depth):
        xp, kp, vp, wb = prompt_layer(l, xp, wb)
        nk_p.append(kp)
        nv_p.append(vp)

    win_t = lambda c: jnp.transpose(c, (0, 1, 3, 4, 2)).reshape(depth, dec_batch, KV_COLS, WINDOW)
    win = lambda c: jnp.transpose(c.reshape(depth, dec_batch, KV_HEADS, HEAD_DIM, WINDOW), (0, 1, 4, 2, 3))
    layer_tile = lambda l, i: (l, i, 0, 0)
    ys, ks, vs, va_s = pl.pallas_call(
        _sample_kernel,
        grid=(depth, s_tiles),
        in_specs=[pl.BlockSpec((s_rows, D_MODEL), lambda l, i: (jnp.where(l == 0, i, s_tiles - 1), 0)),
                  const_spec((s_rows, 3 * LANES)),
                  pl.BlockSpec((None, SAMPLE_BATCH_TILE, KV_COLS, WINDOW), layer_tile),
                  pl.BlockSpec((None, SAMPLE_BATCH_TILE, KV_COLS, WINDOW), layer_tile),
                  layer_spec((KV_HEADS, GQA_GROUP * dec_seq, 1), lambda l, i: l)]
                 + weight_specs(lambda l, i: l, prefetch=SAMPLE_PREFETCH_SHAPES),
        out_specs=[pl.BlockSpec((s_rows, D_MODEL), lambda l, i: (jnp.where(l == depth - 1, i, 0), 0)),
                   pl.BlockSpec((None, SAMPLE_BATCH_TILE, KV_COLS, WINDOW), layer_tile),
                   pl.BlockSpec((None, SAMPLE_BATCH_TILE, KV_COLS, WINDOW), layer_tile),
                   pl.BlockSpec((None, s_rows, A_WIDTH), lambda l, i: (l, i, 0))],
        out_shape=[jax.ShapeDtypeStruct((dec_batch * dec_seq, D_MODEL), _f32),
                   jax.ShapeDtypeStruct((depth, dec_batch, KV_COLS, WINDOW), _f32),
                   jax.ShapeDtypeStruct((depth, dec_batch, KV_COLS, WINDOW), _f32),
                   jax.ShapeDtypeStruct((depth, dec_batch * dec_seq, A_WIDTH), _f32)],
        scratch_shapes=[pltpu.VMEM((s_tiles, s_rows, D_MODEL), _f32)],
        compiler_params=cparams(dimension_semantics=("arbitrary", "arbitrary")),
        name="sample_layers",
    )(x_sample.reshape(dec_batch * dec_seq, D_MODEL), rope_s,
      win_t(cache_win_k), win_t(cache_win_v), sink_rows,
      *head(wb), ones_bd, ws_s.astype(_bf16), bias_s, *tail(wb))

    kv5 = lambda t, n: t.reshape(depth, n, WINDOW, KV_HEADS, HEAD_DIM)
    return (xp, ys.reshape(dec_batch, dec_seq, D_MODEL),
            kv5(jnp.stack(nk_p), batch), kv5(jnp.stack(nv_p), batch), win(ks), win(vs),
            va_s.reshape(depth, dec_batch, dec_seq, A_HEADS, HEAD_DIM))
```

```python
import functools

import jax
import jax.numpy as jnp
import numpy as np
from jax import lax
from jax.experimental import pallas as pl
from jax.experimental.pallas import tpu as pltpu

D_MODEL = 1024
HEAD_DIM = 64
A_WIDTH = 512
B_WIDTH = 512
A_HEADS = A_WIDTH // HEAD_DIM
N_HEADS = B_WIDTH // HEAD_DIM
KV_HEADS = 2
GQA_GROUP = N_HEADS // KV_HEADS
KV_COLS = KV_HEADS * HEAD_DIM
CHUNK = 128
WINDOW = 128
ROPE_THETA = 500000.0
ROT_DIM = HEAD_DIM // 4
D_FF = 4 * D_MODEL
EPS = 1e-6
IN_COLS = 2 * A_WIDTH + B_WIDTH + 2 * KV_COLS
Q_OFF = 2 * A_WIDTH
K_OFF = Q_OFF + B_WIDTH
V_OFF = K_OFF + KV_COLS
NEG_BIG = -1e30
LOG2_E = float(np.log2(np.e))

LANES = 128
FF_CHUNK = 1024
PROMPT_TILE = 512
SAMPLE_BATCH_TILE = 16
PV_LAG = 1
SCORE_LOOKAHEAD = 2
SAMPLE_PREFETCH_SHAPES = ((D_MODEL, IN_COLS), (D_MODEL, D_MODEL), (D_MODEL, D_FF))
FIRST_CAST_STEPS = 8
VMEM_LIMIT_BYTES = 56 * 1024 * 1024

_bf16 = jnp.bfloat16
_f32 = jnp.float32


def _dot(a, b):
    return jnp.dot(a, b, preferred_element_type=_f32)


def _rms(x, g, eps=EPS):
    return x * lax.rsqrt(jnp.mean(x * x, axis=-1, keepdims=True) + eps) * g


def _gelu_x2(x):
    return x * (1.0 + lax.erf(x * np.float32(np.sqrt(0.5))))


def _head_rms(z, g_tiled, ones_bd, eps=EPS):
    ss = _dot((z * z).astype(_bf16), ones_bd)
    return z * lax.rsqrt(ss * (1.0 / HEAD_DIM) + eps) * g_tiled


def _rope(x, cos_t, sin_a, sin_b):
    cols = []
    for c in range(x.shape[1] // LANES):
        xc = x[:, c * LANES:(c + 1) * LANES]
        nxt = pltpu.roll(xc, LANES - ROT_DIM // 2, 1)
        prv = pltpu.roll(xc, ROT_DIM // 2, 1)
        cols.append(xc * cos_t + nxt * sin_a + prv * sin_b)
    return cols[0] if len(cols) == 1 else jnp.concatenate(cols, axis=1)


def _inv_rms(x):
    return lax.rsqrt(jnp.mean(x * x, axis=-1, keepdims=True) + EPS)


def _mixer_inputs_small(x, rope_ref, w_in, g_sv, g_q, g_k, ones_bd):
    z = _dot((x * _inv_rms(x)).astype(_bf16), w_in[...])
    u = _gelu_x2(z[:, :A_WIDTH])
    va = _rms(_gelu_x2(z[:, A_WIDTH:2 * A_WIDTH]), g_sv[...], 4.0 * EPS)
    cos_t, sin_a, sin_b = (rope_ref[:, n * LANES:(n + 1) * LANES] for n in range(3))
    ones = ones_bd[...]
    q = _rope(_head_rms(z[:, Q_OFF:K_OFF], g_q[...], ones), cos_t, sin_a, sin_b)
    k = _rope(_head_rms(z[:, K_OFF:V_OFF], g_k[...], ones[:KV_COLS, :KV_COLS]), cos_t, sin_a, sin_b)
    return u, va, q, k, z[:, V_OFF:]


def _mixer_inputs(x, rope_ref, w_in, g_sv, g_q, g_k, ones_bd):
    xn = x.astype(_bf16)
    mean_sq = jnp.mean(x * x, axis=-1, keepdims=True) + EPS
    r_mix = lax.rsqrt(mean_sq)
    z_qkv = _dot(xn, w_in[:, Q_OFF:])
    cos_t, sin_a, sin_b = (rope_ref[:, n * LANES:(n + 1) * LANES] for n in range(3))
    ones = ones_bd[...]
    head_eps = EPS * mean_sq
    q = _rope(_head_rms(z_qkv[:, :B_WIDTH], g_q[...], ones, head_eps), cos_t, sin_a, sin_b)
    k = _rope(_head_rms(z_qkv[:, B_WIDTH:B_WIDTH + KV_COLS], g_k[...], ones[:KV_COLS, :KV_COLS], head_eps),
              cos_t, sin_a, sin_b)
    v = z_qkv[:, B_WIDTH + KV_COLS:] * r_mix
    va = _rms(_gelu_x2(_dot(xn, w_in[:, A_WIDTH:Q_OFF]) * r_mix), g_sv[...], 4.0 * EPS)
    u = _gelu_x2(_dot(xn, w_in[:, :A_WIDTH]) * r_mix)
    return u, va, q, k, v


def _spatial(va_blk, u_blk, ws_ref, bias):
    vb = va_blk.astype(_bf16)
    outs = [_dot(ws_ref[h], vb[:, h * HEAD_DIM:(h + 1) * HEAD_DIM]) for h in range(A_HEADS)]
    return u_blk * (jnp.concatenate(outs, axis=1) + bias)


def _merge(x, ya, yb, g_oa, g_ob, w_out):
    cat = jnp.concatenate([_rms(ya, g_oa[...]), _rms(yb, g_ob[...])], axis=1).astype(_bf16)
    return x + _dot(cat, w_out[...])


def _ffn(x, w_up, w_down):
    x16 = x.astype(_bf16)
    acc = None
    for c in range(D_FF // FF_CHUNK):
        h = jnp.maximum(_dot(x16, w_up[:, c * FF_CHUNK:(c + 1) * FF_CHUNK]), 0.0)
        part = _dot((h * h).astype(_bf16), w_down[c * FF_CHUNK:(c + 1) * FF_CHUNK, :])
        acc = part if acc is None else acc + part
    r = _inv_rms(x)
    return x + acc * (r * r)


def _cast_slabs(srcs, gains, dsts):
    for src, gain, dst in zip(srcs, gains, dsts):
        dst[...] = (src[...] * gain[...]).astype(_bf16)


def _dot_nt(a, b):
    return lax.dot_general(a, b, (((1,), (1,)), ((), ())), preferred_element_type=_f32)


def _swap_halves(t):
    return pltpu.roll(t, HEAD_DIM, 1)


def _lane_lo(shape):
    return lax.broadcasted_iota(jnp.int32, shape, len(shape) - 1) < HEAD_DIM


def _prompt_kernel(sinks_ref, x_ref, rope_ref, w_in, g_sv, g_q, g_k, ones_bd, ws_ref,
                   bias_ref, g_oa, g_ob, w_out, w_up, w_down, *rest, n_cast):
    cast_src, cast_gain, rest = rest[:n_cast], rest[n_cast:2 * n_cast], rest[2 * n_cast:]
    y_ref, ks_ref, vs_ref = rest[:3]
    cast_dst, (kprev, vprev) = rest[3:3 + n_cast], rest[3 + n_cast:]
    _cast_slabs(cast_src, cast_gain, cast_dst)

    i = pl.program_id(1)
    tile = x_ref.shape[0]
    n_blk = tile // CHUNK

    @pl.when(i == 0)
    def _():
        kprev[...] = jnp.zeros_like(kprev)
        vprev[...] = jnp.zeros_like(vprev)

    x = x_ref[...]
    u, va, q, k, v = _mixer_inputs(x, rope_ref, w_in, g_sv, g_q, g_k, ones_bd)
    even_lanes = lax.broadcasted_iota(jnp.int32, q.shape, 1) % LANES < HEAD_DIM
    q_heads = (jnp.where(even_lanes, q, 0.0).astype(_bf16), jnp.where(even_lanes, 0.0, q).astype(_bf16))
    ks_ref[...] = k[tile - WINDOW:, :]
    vs_ref[...] = v[tile - WINDOW:, :]

    bias = bias_ref[...]
    row = lax.broadcasted_iota(jnp.int32, (CHUNK, 2 * CHUNK), 0)
    col = lax.broadcasted_iota(jnp.int32, (CHUNK, 2 * CHUNK), 1)
    lo = _lane_lo((CHUNK, LANES))

    ones_kv = jnp.ones((2 * CHUNK, LANES), _bf16)
    ya_blocks, yb_blocks = [], []
    for blk in range(n_blk):
        r0, r1 = blk * CHUNK, (blk + 1) * CHUNK
        ya_blocks.append(_spatial(va[r0:r1], u[r0:r1], ws_ref, bias))

        k_blk, v_blk = k[r0:r1], v[r0:r1]
        k_new = (k_blk.astype(_bf16), _swap_halves(k_blk).astype(_bf16))
        v_new = (v_blk.astype(_bf16), _swap_halves(v_blk).astype(_bf16))
        if blk == 0:
            k_old = (kprev[0], kprev[1])
            v_old = (vprev[0], vprev[1])
            first = jnp.where(i == 0, 2 * CHUNK, 0)
        else:
            first = 0
        valid = jnp.where(col < CHUNK, col - row - first, row + CHUNK - col) >= 0
        k_ctx = [jnp.concatenate([k_old[a], k_new[a]], axis=0) for a in range(2)]
        v_ctx = [jnp.concatenate([jnp.concatenate([v_old[a], v_new[a]], axis=0), ones_kv], axis=1)
                 for a in range(2)]
        k_old, v_old = k_new, v_new

        def arrangement(h):
            return (h // GQA_GROUP + h % 2) % 2

        def scores(h):
            qm = q_heads[h % 2][r0:r1, (h // 2) * LANES:(h // 2 + 1) * LANES]
            return _dot_nt(qm, k_ctx[arrangement(h)])

        outs = []
        ahead = [scores(h) for h in range(SCORE_LOOKAHEAD)]
        behind = []

        def finish(hh, e, sink_term):
            o = _dot(e, v_ctx[arrangement(hh)])
            outs.append(o[:, :LANES] * (1.0 / (o[:, LANES:] + sink_term)))

        for h in range(N_HEADS):
            sc = jnp.where(valid, ahead.pop(0), NEG_BIG)
            if h + SCORE_LOOKAHEAD < N_HEADS:
                ahead.append(scores(h + SCORE_LOOKAHEAD))
            sink = sinks_ref[h] * LOG2_E
            m = jnp.maximum(jnp.max(sc, axis=-1, keepdims=True), sink)
            behind.append((h, jnp.exp2(sc - m).astype(_bf16), jnp.exp2(sink - m)))
            if len(behind) > PV_LAG:
                finish(*behind.pop(0))
        while behind:
            finish(*behind.pop(0))
        yb_blocks.append(jnp.concatenate(
            [jnp.where(lo, outs[2 * p], outs[2 * p + 1]) for p in range(N_HEADS // 2)], axis=1))

    kprev[0], kprev[1] = k_old
    vprev[0], vprev[1] = v_old
    ya = jnp.concatenate(ya_blocks, axis=0)
    yb = jnp.concatenate(yb_blocks, axis=0)
    y_ref[...] = _ffn(_merge(x, ya, yb, g_oa, g_ob, w_out), w_up, w_down)


def _sample_kernel(x_ref, rope_ref, ck_ref, cv_ref, sink_rows, w_in, g_sv, g_q, g_k, ones_bd,
                   ws_ref, bias_ref, g_oa, g_ob, w_out, w_up, w_down,
                   y_ref, ko_ref, vo_ref, va_ref, x_all):
    l = pl.program_id(0)
    i = pl.program_id(1)
    rows = x_ref.shape[0]
    nb = ck_ref.shape[0]
    dec_seq = rows // nb
    grp_rows = GQA_GROUP * dec_seq

    @pl.when(l == 0)
    def _():
        x_all[i] = x_ref[...]

    x = x_all[i]
    u, va, q, k, v = _mixer_inputs_small(x, rope_ref, w_in, g_sv, g_q, g_k, ones_bd)
    for h in range(A_HEADS):
        va_ref[:, h, :] = va[:, h * HEAD_DIM:(h + 1) * HEAD_DIM]

    bias = bias_ref[...]
    ya = jnp.concatenate(
        [_spatial(va[r:r + CHUNK], u[r:r + CHUNK], ws_ref, bias) for r in range(0, rows, CHUNK)], axis=0)

    ck, cv = ck_ref[...], cv_ref[...]
    k_t, v_t = k.T, v.T
    lane = lax.broadcasted_iota(jnp.int32, (KV_COLS, WINDOW), 1)
    for b in range(nb):
        col = b * dec_seq
        grp, off = col // LANES, col % LANES
        for new_t, cache, out_ref in ((k_t, ck, ko_ref), (v_t, cv, vo_ref)):
            fresh = pltpu.roll(new_t[:, grp * LANES:(grp + 1) * LANES], (LANES - off) % LANES, 1)
            merged = jnp.where(lane < dec_seq, fresh, cache[b])
            out_ref[b] = pltpu.roll(merged, WINDOW - dec_seq, 1)

    ck16, cv16 = ck.astype(_bf16), cv.astype(_bf16)
    kt16 = k_t.astype(_bf16)
    q3 = q.reshape(nb, dec_seq, B_WIDTH)
    t_c = lax.broadcasted_iota(jnp.int32, (grp_rows, WINDOW), 0) % dec_seq
    w_c = lax.broadcasted_iota(jnp.int32, (grp_rows, WINDOW), 1)
    valid_c = (w_c >= t_c)[None]
    b_n = lax.broadcasted_iota(jnp.int32, (nb, grp_rows, rows), 0)
    t_n = lax.broadcasted_iota(jnp.int32, (nb, grp_rows, rows), 1) % dec_seq
    c_n = lax.broadcasted_iota(jnp.int32, (nb, grp_rows, rows), 2)
    t_key = c_n - b_n * dec_seq
    valid_n = jnp.minimum(t_key, t_n - t_key) >= 0

    heads = []
    for kv in range(KV_HEADS):
        d0, d1 = kv * HEAD_DIM, (kv + 1) * HEAD_DIM
        q_st = jnp.concatenate(
            [q3[:, :, h * HEAD_DIM:(h + 1) * HEAD_DIM] for h in range(kv * GQA_GROUP, (kv + 1) * GQA_GROUP)],
            axis=1).astype(_bf16)
        s_c = jnp.einsum('bqd,bdw->bqw', q_st, ck16[:, d0:d1, :], preferred_element_type=_f32)
        s_n = _dot(q_st.reshape(nb * grp_rows, HEAD_DIM), kt16[d0:d1, :]).reshape(nb, grp_rows, rows)
        s_c = jnp.where(valid_c, s_c, NEG_BIG)
        s_n = jnp.where(valid_n, s_n, NEG_BIG)
        sink = sink_rows[kv][None]
        m = jnp.maximum(jnp.maximum(jnp.max(s_c, axis=-1, keepdims=True),
                                    jnp.max(s_n, axis=-1, keepdims=True)), sink)
        e_c, e_n = jnp.exp2(s_c - m), jnp.exp2(s_n - m)
        denom = (jnp.sum(e_c, axis=-1, keepdims=True) + jnp.sum(e_n, axis=-1, keepdims=True)
                 + jnp.exp2(sink - m))
        o = jnp.einsum('bqw,bdw->bqd', e_c.astype(_bf16), cv16[:, d0:d1, :], preferred_element_type=_f32)
        o = o + _dot(e_n.reshape(nb * grp_rows, rows).astype(_bf16),
                     v[:, d0:d1].astype(_bf16)).reshape(nb, grp_rows, HEAD_DIM)
        o = o * (1.0 / denom)
        heads += [o[:, g * dec_seq:(g + 1) * dec_seq, :] for g in range(GQA_GROUP)]
    yb = jnp.concatenate(heads, axis=2).reshape(rows, B_WIDTH)
    y = _ffn(_merge(x, ya, yb, g_oa, g_ob, w_out), w_up, w_down)
    x_all[i] = y
    y_ref[...] = y


def _rope_tables(pos):
    half = ROT_DIM // 2
    inv = np.power(np.float32(ROPE_THETA), -2.0 * np.arange(half, dtype=np.float32) / np.float32(ROT_DIM))
    ang = pos.astype(np.float32)[:, None] * inv.astype(np.float32)[None, :]
    cos, sin = np.cos(ang).astype(np.float32), np.sin(ang).astype(np.float32)
    n = pos.shape[0]
    pad1 = np.ones((n, HEAD_DIM - ROT_DIM), np.float32)
    pad0 = np.zeros((n, HEAD_DIM - ROT_DIM), np.float32)
    zero = np.zeros((n, half), np.float32)
    cos_t = np.concatenate([cos, cos, pad1], axis=1)
    sin_a = np.concatenate([-sin, zero, pad0], axis=1)
    sin_b = np.concatenate([zero, sin, pad0], axis=1)
    rep = LANES // HEAD_DIM
    return np.concatenate([np.tile(t, (1, rep)) for t in (cos_t, sin_a, sin_b)], axis=1)


def kernel(x_prompt, x_sample, cache_win_k, cache_win_v, g_mix, w_in, g_sv, g_q, g_k, w_spatial,
           b_spatial, sinks, g_out_a, g_out_b, w_out, g_ffn, w_up, w_down):
    batch, seq, _ = x_prompt.shape
    dec_batch, dec_seq, _ = x_sample.shape
    depth = w_in.shape[0]
    assert seq % PROMPT_TILE == 0 and PROMPT_TILE % CHUNK == 0
    assert dec_batch % SAMPLE_BATCH_TILE == 0 and CHUNK % dec_seq == 0
    s_rows = SAMPLE_BATCH_TILE * dec_seq
    assert s_rows % CHUNK == 0
    n_tiles = seq // PROMPT_TILE
    s_tiles = dec_batch // SAMPLE_BATCH_TILE

    rope_p = jnp.asarray(_rope_tables(np.arange(seq)))
    rope_s = jnp.asarray(np.tile(_rope_tables(seq + np.arange(dec_seq)), (SAMPLE_BATCH_TILE, 1)))
    hid = np.arange(B_WIDTH) // HEAD_DIM
    ones_bd = jnp.asarray(hid[:, None] == hid[None, :], dtype=_bf16)
    tril = np.tril(np.ones((CHUNK, CHUNK), dtype=bool))
    rep = CHUNK // dec_seq
    expand = jnp.asarray(np.tile(np.eye(dec_seq, dtype=np.float32), (rep, 1)))
    same_batch = jnp.asarray(np.kron(np.eye(rep), np.ones((dec_seq, dec_seq))), dtype=_f32)

    row3 = lambda a: a[:, None, :]
    ws = jnp.where(tril, 0.5 * w_spatial, 0.0)
    bias = jnp.repeat(jnp.swapaxes(0.5 * b_spatial, 1, 2), HEAD_DIM, axis=2)
    ws_s = jnp.einsum('rt,lhts,cs->lhrc', expand, ws[:, :, :dec_seq, :dec_seq], expand,
                      precision=lax.Precision.HIGHEST) * same_batch
    bias_s = jnp.tile(bias[:, :dec_seq], (1, rep, 1))
    sink_rows = jnp.repeat((sinks * LOG2_E).reshape(depth, KV_HEADS, GQA_GROUP), dec_seq, axis=2)[..., None]
    big_f32 = (w_in, w_out, w_up, w_down)
    col3 = lambda a: a[:, :, None]
    big_gain = (col3(g_mix), jnp.ones((depth, D_MODEL, 1), _f32), col3(g_ffn), jnp.ones((depth, D_FF, 1), _f32))
    g_q_scaled = jnp.tile(g_q, (1, N_HEADS)) * (HEAD_DIM ** -0.5 * LOG2_E)
    head = lambda wb: (wb[0], row3(g_sv), row3(g_q_scaled), row3(jnp.tile(g_k, (1, KV_HEADS))))
    tail = lambda wb: (row3(g_out_a), row3(g_out_b), wb[1], wb[2], wb[3])
    head_shapes = [(D_MODEL, IN_COLS), (1, A_WIDTH), (1, B_WIDTH), (1, KV_COLS)]
    tail_shapes = [(1, A_WIDTH), (1, B_WIDTH), (D_MODEL, D_MODEL), (D_MODEL, D_FF), (D_FF, D_MODEL)]
    spatial_shapes = [(A_HEADS, CHUNK, CHUNK), (CHUNK, A_WIDTH)]

    def const_spec(shape):
        nd = len(shape)
        return pl.BlockSpec(shape, lambda *_: (0,) * nd, pipeline_mode=pl.Buffered(1))

    def layer_spec(shape, layer_of, buffers=1):
        nd = len(shape)
        return pl.BlockSpec((None,) + tuple(shape), lambda *g: (layer_of(*g),) + (0,) * nd,
                            pipeline_mode=pl.Buffered(buffers))

    def weight_specs(layer_of, prefetch=()):
        spec = lambda sh: layer_spec(sh, layer_of, 2 if sh in prefetch else 1)
        return ([spec(sh) for sh in head_shapes] + [const_spec((B_WIDTH, B_WIDTH))]
                + [spec(sh) for sh in spatial_shapes + tail_shapes])

    cparams = functools.partial(pltpu.CompilerParams, vmem_limit_bytes=VMEM_LIMIT_BYTES)

    n_slabs = batch * n_tiles
    assert all(w.shape[1] % (n_slabs * 16) == 0 for w in big_f32)
    slab_shape = lambda w, n=n_slabs: (None, w.shape[1] // n, w.shape[2])
    n_big = len(big_f32)

    def cast_first_layer_kernel(*refs):
        _cast_slabs(refs[:n_big], refs[n_big:2 * n_big], refs[2 * n_big:])

    wb = pl.pallas_call(
        cast_first_layer_kernel,
        grid=(FIRST_CAST_STEPS,),
        in_specs=[pl.BlockSpec(slab_shape(w, FIRST_CAST_STEPS), lambda s: (0, s, 0)) for w in big_f32 + big_gain],
        out_specs=[pl.BlockSpec(slab_shape(w, FIRST_CAST_STEPS), lambda s: (0, s, 0)) for w in big_f32],
        out_shape=[jax.ShapeDtypeStruct(w.shape, _bf16) for w in big_f32],
        compiler_params=cparams(dimension_semantics=("arbitrary",)),
        name="cast_first_layer",
    )(*big_f32, *big_gain)

    n_fixed_in = 3 + len(head_shapes) + 1 + len(spatial_shapes) + len(tail_shapes)
    big_in_pos = {0: 3, 1: 12, 2: 13, 3: 14}

    def prompt_layer(l, xp, wb):
        cast_next = l + 1 < depth
        slab_of = lambda b, i: (l + 1, b * n_tiles + i, 0)
        in_specs = ([pl.BlockSpec(memory_space=pltpu.SMEM),
                     pl.BlockSpec((None, PROMPT_TILE, D_MODEL), lambda b, i: (b, i, 0)),
                     pl.BlockSpec((PROMPT_TILE, 3 * LANES), lambda b, i: (i, 0))]
                    + weight_specs(lambda b, i: l))
        assert len(in_specs) == n_fixed_in
        out_specs = [pl.BlockSpec((None, PROMPT_TILE, D_MODEL), lambda b, i: (b, i, 0)),
                     pl.BlockSpec((None, WINDOW, KV_COLS), lambda b, i: (b, 0, 0)),
                     pl.BlockSpec((None, WINDOW, KV_COLS), lambda b, i: (b, 0, 0))]
        out_shape = [jax.ShapeDtypeStruct((batch, seq, D_MODEL), _f32),
                     jax.ShapeDtypeStruct((batch, WINDOW, KV_COLS), _f32),
                     jax.ShapeDtypeStruct((batch, WINDOW, KV_COLS), _f32)]
        args = (sinks[l], xp, rope_p, *head(wb), ones_bd, ws.astype(_bf16), bias, *tail(wb))
        aliases = {}
        if cast_next:
            in_specs += [pl.BlockSpec(slab_shape(w), slab_of) for w in big_f32 + big_gain]
            out_specs += [pl.BlockSpec(slab_shape(w), slab_of) for w in big_f32]
            out_shape += [jax.ShapeDtypeStruct(w.shape, _bf16) for w in big_f32]
            args += big_f32 + big_gain
            aliases = {big_in_pos[j]: 3 + j for j in range(n_big)}
        outs = pl.pallas_call(
            functools.partial(_prompt_kernel, n_cast=n_big if cast_next else 0),
            grid=(batch, n_tiles),
            in_specs=in_specs, out_specs=out_specs, out_shape=out_shape,
            input_output_aliases=aliases,
            scratch_shapes=[pltpu.VMEM((2, CHUNK, KV_COLS), _bf16), pltpu.VMEM((2, CHUNK, KV_COLS), _bf16)],
            compiler_params=cparams(dimension_semantics=("arbitrary", "arbitrary")),
            name="prompt_layer",
        )(*args)
        return outs[0], outs[1], outs[2], (tuple(outs[3:]) if cast_next else wb)

    xp = x_prompt
    nk_p, nv_p = [], []
    for l in range(depth):
        xp, kp, vp, wb = prompt_layer(l, xp, wb)
        nk_p.append(kp)
        nv_p.append(vp)

    win_t = lambda c: jnp.transpose(c, (0, 1, 3, 4, 2)).reshape(depth, dec_batch, KV_COLS, WINDOW)
    win = lambda c: jnp.transpose(c.reshape(depth, dec_batch, KV_HEADS, HEAD_DIM, WINDOW), (0, 1, 4, 2, 3))
    layer_tile = lambda l, i: (l, i, 0, 0)
    ys, ks, vs, va_s = pl.pallas_call(
        _sample_kernel,
        grid=(depth, s_tiles),
        in_specs=[pl.BlockSpec((s_rows, D_MODEL), lambda l, i: (jnp.where(l == 0, i, s_tiles - 1), 0)),
                  const_spec((s_rows, 3 * LANES)),
                  pl.BlockSpec((None, SAMPLE_BATCH_TILE, KV_COLS, WINDOW), layer_tile),
                  pl.BlockSpec((None, SAMPLE_BATCH_TILE, KV_COLS, WINDOW), layer_tile),
                  layer_spec((KV_HEADS, GQA_GROUP * dec_seq, 1), lambda l, i: l)]
                 + weight_specs(lambda l, i: l, prefetch=SAMPLE_PREFETCH_SHAPES),
        out_specs=[pl.BlockSpec((s_rows, D_MODEL), lambda l, i: (jnp.where(l == depth - 1, i, 0), 0)),
                   pl.BlockSpec((None, SAMPLE_BATCH_TILE, KV_COLS, WINDOW), layer_tile),
                   pl.BlockSpec((None, SAMPLE_BATCH_TILE, KV_COLS, WINDOW), layer_tile),
                   pl.BlockSpec((None, s_rows, A_HEADS, HEAD_DIM), lambda l, i: (l, i, 0, 0))],
        out_shape=[jax.ShapeDtypeStruct((dec_batch * dec_seq, D_MODEL), _f32),
                   jax.ShapeDtypeStruct((depth, dec_batch, KV_COLS, WINDOW), _f32),
                   jax.ShapeDtypeStruct((depth, dec_batch, KV_COLS, WINDOW), _f32),
                   jax.ShapeDtypeStruct((depth, dec_batch * dec_seq, A_HEADS, HEAD_DIM), _f32)],
        scratch_shapes=[pltpu.VMEM((s_tiles, s_rows, D_MODEL), _f32)],
        compiler_params=cparams(dimension_semantics=("arbitrary", "arbitrary")),
        name="sample_layers",
    )(x_sample.reshape(dec_batch * dec_seq, D_MODEL), rope_s,
      win_t(cache_win_k), win_t(cache_win_v), sink_rows,
      *head(wb), ones_bd, ws_s.astype(_bf16), bias_s, *tail(wb))

    kv5 = lambda t, n: t.reshape(depth, n, WINDOW, KV_HEADS, HEAD_DIM)
    return (xp, ys.reshape(dec_batch, dec_seq, D_MODEL),
            kv5(jnp.stack(nk_p), batch), kv5(jnp.stack(nv_p), batch), win(ks), win(vs),
            va_s.reshape(depth, dec_batch, dec_seq, A_HEADS, HEAD_DIM))
```

```python
import functools

import jax
import jax.numpy as jnp
import numpy as np
from jax import lax
from jax.experimental import pallas as pl
from jax.experimental.pallas import tpu as pltpu

D_MODEL = 1024
HEAD_DIM = 64
A_WIDTH = 512
B_WIDTH = 512
A_HEADS = A_WIDTH // HEAD_DIM
N_HEADS = B_WIDTH // HEAD_DIM
KV_HEADS = 2
GQA_GROUP = N_HEADS // KV_HEADS
KV_COLS = KV_HEADS * HEAD_DIM
CHUNK = 128
WINDOW = 128
ROPE_THETA = 500000.0
ROT_DIM = HEAD_DIM // 4
D_FF = 4 * D_MODEL
EPS = 1e-6
IN_COLS = 2 * A_WIDTH + B_WIDTH + 2 * KV_COLS
Q_OFF = 2 * A_WIDTH
K_OFF = Q_OFF + B_WIDTH
V_OFF = K_OFF + KV_COLS
NEG_BIG = -1e30
LOG2_E = float(np.log2(np.e))

LANES = 128
FF_CHUNK = 1024
PROMPT_TILE = 512
SAMPLE_BATCH_TILE = 16
PV_LAG = 1
SCORE_LOOKAHEAD = 2
SAMPLE_PREFETCH_SHAPES = ((D_MODEL, IN_COLS), (D_MODEL, D_MODEL), (D_MODEL, D_FF))
FIRST_CAST_STEPS = 8
VMEM_LIMIT_BYTES = 56 * 1024 * 1024

_bf16 = jnp.bfloat16
_f32 = jnp.float32


def _dot(a, b):
    return jnp.dot(a, b, preferred_element_type=_f32)


def _rms(x, g, eps=EPS):
    return x * lax.rsqrt(jnp.mean(x * x, axis=-1, keepdims=True) + eps) * g


def _gelu_x2(x):
    return x * (1.0 + lax.erf(x * np.float32(np.sqrt(0.5))))


def _head_rms(z, g_tiled, ones_bd, eps=EPS):
    ss = _dot((z * z).astype(_bf16), ones_bd)
    return z * lax.rsqrt(ss * (1.0 / HEAD_DIM) + eps) * g_tiled


def _rope(x, cos_t, sin_a, sin_b):
    cols = []
    for c in range(x.shape[1] // LANES):
        xc = x[:, c * LANES:(c + 1) * LANES]
        nxt = pltpu.roll(xc, LANES - ROT_DIM // 2, 1)
        prv = pltpu.roll(xc, ROT_DIM // 2, 1)
        cols.append(xc * cos_t + nxt * sin_a + prv * sin_b)
    return cols[0] if len(cols) == 1 else jnp.concatenate(cols, axis=1)


def _inv_rms(x):
    return lax.rsqrt(jnp.mean(x * x, axis=-1, keepdims=True) + EPS)


def _mixer_inputs_small(x, rope_ref, w_in, g_sv, g_q, g_k, ones_bd):
    z = _dot((x * _inv_rms(x)).astype(_bf16), w_in[...])
    u = _gelu_x2(z[:, :A_WIDTH])
    va = _rms(_gelu_x2(z[:, A_WIDTH:2 * A_WIDTH]), g_sv[...], 4.0 * EPS)
    cos_t, sin_a, sin_b = (rope_ref[:, n * LANES:(n + 1) * LANES] for n in range(3))
    ones = ones_bd[...]
    q = _rope(_head_rms(z[:, Q_OFF:K_OFF], g_q[...], ones), cos_t, sin_a, sin_b)
    k = _rope(_head_rms(z[:, K_OFF:V_OFF], g_k[...], ones[:KV_COLS, :KV_COLS]), cos_t, sin_a, sin_b)
    return u, va, q, k, z[:, V_OFF:]


def _mixer_inputs(x, rope_ref, w_in, g_sv, g_q, g_k, ones_bd):
    xn = x.astype(_bf16)
    mean_sq = jnp.mean(x * x, axis=-1, keepdims=True) + EPS
    r_mix = lax.rsqrt(mean_sq)
    z_qkv = _dot(xn, w_in[:, Q_OFF:])
    cos_t, sin_a, sin_b = (rope_ref[:, n * LANES:(n + 1) * LANES] for n in range(3))
    ones = ones_bd[...]
    head_eps = EPS * mean_sq
    q = _rope(_head_rms(z_qkv[:, :B_WIDTH], g_q[...], ones, head_eps), cos_t, sin_a, sin_b)
    k = _rope(_head_rms(z_qkv[:, B_WIDTH:B_WIDTH + KV_COLS], g_k[...], ones[:KV_COLS, :KV_COLS], head_eps),
              cos_t, sin_a, sin_b)
    v = z_qkv[:, B_WIDTH + KV_COLS:] * r_mix
    va = _rms(_gelu_x2(_dot(xn, w_in[:, A_WIDTH:Q_OFF]) * r_mix), g_sv[...], 4.0 * EPS)
    u = _gelu_x2(_dot(xn, w_in[:, :A_WIDTH]) * r_mix)
    return u, va, q, k, v


def _spatial(va_blk, u_blk, ws_ref, bias):
    vb = va_blk.astype(_bf16)
    outs = [_dot(ws_ref[h], vb[:, h * HEAD_DIM:(h + 1) * HEAD_DIM]) for h in range(A_HEADS)]
    return u_blk * (jnp.concatenate(outs, axis=1) + bias)


def _merge(x, ya, yb, g_oa, g_ob, w_out):
    cat = jnp.concatenate([_rms(ya, g_oa[...]), _rms(yb, g_ob[...])], axis=1).astype(_bf16)
    return x + _dot(cat, w_out[...])


def _ffn(x, w_up, w_down):
    x16 = x.astype(_bf16)
    acc = None
    for c in range(D_FF // FF_CHUNK):
        h = jnp.maximum(_dot(x16, w_up[:, c * FF_CHUNK:(c + 1) * FF_CHUNK]), 0.0)
        part = _dot((h * h).astype(_bf16), w_down[c * FF_CHUNK:(c + 1) * FF_CHUNK, :])
        acc = part if acc is None else acc + part
    r = _inv_rms(x)
    return x + acc * (r * r)


GAINED = (0, 2)


def _cast_slabs(srcs, gains, dsts):
    gain_of = dict(zip(GAINED, gains))
    for j, (src, dst) in enumerate(zip(srcs, dsts)):
        val = src[...] * gain_of[j][...] if j in gain_of else src[...]
        dst[...] = val.astype(_bf16)


def _dot_nt(a, b):
    return lax.dot_general(a, b, (((1,), (1,)), ((), ())), preferred_element_type=_f32)


def _swap_halves(t):
    return pltpu.roll(t, HEAD_DIM, 1)


def _lane_lo(shape):
    return lax.broadcasted_iota(jnp.int32, shape, len(shape) - 1) < HEAD_DIM


def _prompt_kernel(sinks_ref, x_ref, rope_ref, w_in, g_sv, g_q, g_k, ones_bd, ws_ref,
                   bias_ref, g_oa, g_ob, w_out, w_up, w_down, *rest, n_cast):
    n_gain = len(GAINED) if n_cast else 0
    cast_src, cast_gain, rest = rest[:n_cast], rest[n_cast:n_cast + n_gain], rest[n_cast + n_gain:]
    y_ref, ks_ref, vs_ref = rest[:3]
    cast_dst, (kprev, vprev) = rest[3:3 + n_cast], rest[3 + n_cast:]
    _cast_slabs(cast_src, cast_gain, cast_dst)

    i = pl.program_id(1)
    tile = x_ref.shape[0]
    n_blk = tile // CHUNK

    @pl.when(i == 0)
    def _():
        kprev[...] = jnp.zeros_like(kprev)
        vprev[...] = jnp.zeros_like(vprev)

    x = x_ref[...]
    u, va, q, k, v = _mixer_inputs(x, rope_ref, w_in, g_sv, g_q, g_k, ones_bd)
    even_lanes = lax.broadcasted_iota(jnp.int32, q.shape, 1) % LANES < HEAD_DIM
    q_heads = (jnp.where(even_lanes, q, 0.0).astype(_bf16), jnp.where(even_lanes, 0.0, q).astype(_bf16))
    ks_ref[...] = k[tile - WINDOW:, :]
    vs_ref[...] = v[tile - WINDOW:, :]

    bias = bias_ref[...]
    row = lax.broadcasted_iota(jnp.int32, (CHUNK, 2 * CHUNK), 0)
    col = lax.broadcasted_iota(jnp.int32, (CHUNK, 2 * CHUNK), 1)
    lo = _lane_lo((CHUNK, LANES))

    ones_kv = jnp.ones((2 * CHUNK, LANES), _bf16)
    ya_blocks, yb_blocks = [], []
    for blk in range(n_blk):
        r0, r1 = blk * CHUNK, (blk + 1) * CHUNK
        ya_blocks.append(_spatial(va[r0:r1], u[r0:r1], ws_ref, bias))

        k_blk, v_blk = k[r0:r1], v[r0:r1]
        k_new = (k_blk.astype(_bf16), _swap_halves(k_blk).astype(_bf16))
        v_new = (v_blk.astype(_bf16), _swap_halves(v_blk).astype(_bf16))
        if blk == 0:
            k_old = (kprev[0], kprev[1])
            v_old = (vprev[0], vprev[1])
            first = jnp.where(i == 0, 2 * CHUNK, 0)
        else:
            first = 0
        valid = jnp.where(col < CHUNK, col - row - first, row + CHUNK - col) >= 0
        k_ctx = [jnp.concatenate([k_old[a], k_new[a]], axis=0) for a in range(2)]
        v_ctx = [jnp.concatenate([jnp.concatenate([v_old[a], v_new[a]], axis=0), ones_kv], axis=1)
                 for a in range(2)]
        k_old, v_old = k_new, v_new

        def arrangement(h):
            return (h // GQA_GROUP + h % 2) % 2

        def scores(h):
            qm = q_heads[h % 2][r0:r1, (h // 2) * LANES:(h // 2 + 1) * LANES]
            return _dot_nt(qm, k_ctx[arrangement(h)])

        outs = []
        ahead = [scores(h) for h in range(SCORE_LOOKAHEAD)]
        behind = []

        def finish(hh, e, sink_term):
            o = _dot(e, v_ctx[arrangement(hh)])
            outs.append(o[:, :LANES] * (1.0 / (o[:, LANES:] + sink_term)))

        for h in range(N_HEADS):
            sc = jnp.where(valid, ahead.pop(0), NEG_BIG)
            if h + SCORE_LOOKAHEAD < N_HEADS:
                ahead.append(scores(h + SCORE_LOOKAHEAD))
            sink = sinks_ref[h] * LOG2_E
            m = jnp.maximum(jnp.max(sc, axis=-1, keepdims=True), sink)
            behind.append((h, jnp.exp2(sc - m).astype(_bf16), jnp.exp2(sink - m)))
            if len(behind) > PV_LAG:
                finish(*behind.pop(0))
        while behind:
            finish(*behind.pop(0))
        yb_blocks.append(jnp.concatenate(
            [jnp.where(lo, outs[2 * p], outs[2 * p + 1]) for p in range(N_HEADS // 2)], axis=1))

    kprev[0], kprev[1] = k_old
    vprev[0], vprev[1] = v_old
    ya = jnp.concatenate(ya_blocks, axis=0)
    yb = jnp.concatenate(yb_blocks, axis=0)
    y_ref[...] = _ffn(_merge(x, ya, yb, g_oa, g_ob, w_out), w_up, w_down)


def _sample_kernel(x_ref, rope_ref, ck_ref, cv_ref, sink_rows, w_in, g_sv, g_q, g_k, ones_bd,
                   ws_ref, bias_ref, g_oa, g_ob, w_out, w_up, w_down,
                   y_ref, ko_ref, vo_ref, va_ref, x_all):
    l = pl.program_id(0)
    i = pl.program_id(1)
    rows = x_ref.shape[0]
    nb = ck_ref.shape[0]
    dec_seq = rows // nb
    grp_rows = GQA_GROUP * dec_seq

    @pl.when(l == 0)
    def _():
        x_all[i] = x_ref[...]

    x = x_all[i]
    u, va, q, k, v = _mixer_inputs_small(x, rope_ref, w_in, g_sv, g_q, g_k, ones_bd)
    for h in range(A_HEADS):
        va_ref[:, h, :] = va[:, h * HEAD_DIM:(h + 1) * HEAD_DIM]

    bias = bias_ref[...]
    ya = jnp.concatenate(
        [_spatial(va[r:r + CHUNK], u[r:r + CHUNK], ws_ref, bias) for r in range(0, rows, CHUNK)], axis=0)

    ck, cv = ck_ref[...], cv_ref[...]
    k_t, v_t = k.T, v.T
    lane = lax.broadcasted_iota(jnp.int32, (KV_COLS, WINDOW), 1)
    for b in range(nb):
        col = b * dec_seq
        grp, off = col // LANES, col % LANES
        for new_t, cache, out_ref in ((k_t, ck, ko_ref), (v_t, cv, vo_ref)):
            fresh = pltpu.roll(new_t[:, grp * LANES:(grp + 1) * LANES], (LANES - off) % LANES, 1)
            merged = jnp.where(lane < dec_seq, fresh, cache[b])
            out_ref[b] = pltpu.roll(merged, WINDOW - dec_seq, 1)

    ck16, cv16 = ck.astype(_bf16), cv.astype(_bf16)
    kt16 = k_t.astype(_bf16)
    q3 = q.reshape(nb, dec_seq, B_WIDTH)
    t_c = lax.broadcasted_iota(jnp.int32, (grp_rows, WINDOW), 0) % dec_seq
    w_c = lax.broadcasted_iota(jnp.int32, (grp_rows, WINDOW), 1)
    valid_c = (w_c >= t_c)[None]
    b_n = lax.broadcasted_iota(jnp.int32, (nb, grp_rows, rows), 0)
    t_n = lax.broadcasted_iota(jnp.int32, (nb, grp_rows, rows), 1) % dec_seq
    c_n = lax.broadcasted_iota(jnp.int32, (nb, grp_rows, rows), 2)
    t_key = c_n - b_n * dec_seq
    valid_n = jnp.minimum(t_key, t_n - t_key) >= 0

    heads = []
    for kv in range(KV_HEADS):
        d0, d1 = kv * HEAD_DIM, (kv + 1) * HEAD_DIM
        q_st = jnp.concatenate(
            [q3[:, :, h * HEAD_DIM:(h + 1) * HEAD_DIM] for h in range(kv * GQA_GROUP, (kv + 1) * GQA_GROUP)],
            axis=1).astype(_bf16)
        s_c = jnp.einsum('bqd,bdw->bqw', q_st, ck16[:, d0:d1, :], preferred_element_type=_f32)
        s_n = _dot(q_st.reshape(nb * grp_rows, HEAD_DIM), kt16[d0:d1, :]).reshape(nb, grp_rows, rows)
        s_c = jnp.where(valid_c, s_c, NEG_BIG)
        s_n = jnp.where(valid_n, s_n, NEG_BIG)
        sink = sink_rows[kv][None]
        m = jnp.maximum(jnp.maximum(jnp.max(s_c, axis=-1, keepdims=True),
                                    jnp.max(s_n, axis=-1, keepdims=True)), sink)
        e_c, e_n = jnp.exp2(s_c - m), jnp.exp2(s_n - m)
        denom = (jnp.sum(e_c, axis=-1, keepdims=True) + jnp.sum(e_n, axis=-1, keepdims=True)
                 + jnp.exp2(sink - m))
        o = jnp.einsum('bqw,bdw->bqd', e_c.astype(_bf16), cv16[:, d0:d1, :], preferred_element_type=_f32)
        o = o + _dot(e_n.reshape(nb * grp_rows, rows).astype(_bf16),
                     v[:, d0:d1].astype(_bf16)).reshape(nb, grp_rows, HEAD_DIM)
        o = o * (1.0 / denom)
        heads += [o[:, g * dec_seq:(g + 1) * dec_seq, :] for g in range(GQA_GROUP)]
    yb = jnp.concatenate(heads, axis=2).reshape(rows, B_WIDTH)
    y = _ffn(_merge(x, ya, yb, g_oa, g_ob, w_out), w_up, w_down)
    x_all[i] = y
    y_ref[...] = y


def _rope_tables(pos):
    half = ROT_DIM // 2
    inv = np.power(np.float32(ROPE_THETA), -2.0 * np.arange(half, dtype=np.float32) / np.float32(ROT_DIM))
    ang = pos.astype(np.float32)[:, None] * inv.astype(np.float32)[None, :]
    cos, sin = np.cos(ang).astype(np.float32), np.sin(ang).astype(np.float32)
    n = pos.shape[0]
    pad1 = np.ones((n, HEAD_DIM - ROT_DIM), np.float32)
    pad0 = np.zeros((n, HEAD_DIM - ROT_DIM), np.float32)
    zero = np.zeros((n, half), np.float32)
    cos_t = np.concatenate([cos, cos, pad1], axis=1)
    sin_a = np.concatenate([-sin, zero, pad0], axis=1)
    sin_b = np.concatenate([zero, sin, pad0], axis=1)
    rep = LANES // HEAD_DIM
    return np.concatenate([np.tile(t, (1, rep)) for t in (cos_t, sin_a, sin_b)], axis=1)


def kernel(x_prompt, x_sample, cache_win_k, cache_win_v, g_mix, w_in, g_sv, g_q, g_k, w_spatial,
           b_spatial, sinks, g_out_a, g_out_b, w_out, g_ffn, w_up, w_down):
    batch, seq, _ = x_prompt.shape
    dec_batch, dec_seq, _ = x_sample.shape
    depth = w_in.shape[0]
    assert seq % PROMPT_TILE == 0 and PROMPT_TILE % CHUNK == 0
    assert dec_batch % SAMPLE_BATCH_TILE == 0 and CHUNK % dec_seq == 0
    s_rows = SAMPLE_BATCH_TILE * dec_seq
    assert s_rows % CHUNK == 0
    n_tiles = seq // PROMPT_TILE
    s_tiles = dec_batch // SAMPLE_BATCH_TILE

    rope_p = jnp.asarray(_rope_tables(np.arange(seq)))
    rope_s = jnp.asarray(np.tile(_rope_tables(seq + np.arange(dec_seq)), (SAMPLE_BATCH_TILE, 1)))
    hid = np.arange(B_WIDTH) // HEAD_DIM
    ones_bd = jnp.asarray(hid[:, None] == hid[None, :], dtype=_bf16)
    tril = np.tril(np.ones((CHUNK, CHUNK), dtype=bool))
    rep = CHUNK // dec_seq
    expand = jnp.asarray(np.tile(np.eye(dec_seq, dtype=np.float32), (rep, 1)))
    same_batch = jnp.asarray(np.kron(np.eye(rep), np.ones((dec_seq, dec_seq))), dtype=_f32)

    row3 = lambda a: a[:, None, :]
    ws = jnp.where(tril, 0.5 * w_spatial, 0.0)
    bias = jnp.repeat(jnp.swapaxes(0.5 * b_spatial, 1, 2), HEAD_DIM, axis=2)
    ws_s = jnp.einsum('rt,lhts,cs->lhrc', expand, ws[:, :, :dec_seq, :dec_seq], expand,
                      precision=lax.Precision.HIGHEST) * same_batch
    bias_s = jnp.tile(bias[:, :dec_seq], (1, rep, 1))
    sink_rows = jnp.repeat((sinks * LOG2_E).reshape(depth, KV_HEADS, GQA_GROUP), dec_seq, axis=2)[..., None]
    big_f32 = (w_in, w_out, w_up, w_down)
    col3 = lambda a: a[:, :, None]
    big_gain = (col3(g_mix), col3(g_ffn))
    g_q_scaled = jnp.tile(g_q, (1, N_HEADS)) * (HEAD_DIM ** -0.5 * LOG2_E)
    head = lambda wb: (wb[0], row3(g_sv), row3(g_q_scaled), row3(jnp.tile(g_k, (1, KV_HEADS))))
    tail = lambda wb: (row3(g_out_a), row3(g_out_b), wb[1], wb[2], wb[3])
    head_shapes = [(D_MODEL, IN_COLS), (1, A_WIDTH), (1, B_WIDTH), (1, KV_COLS)]
    tail_shapes = [(1, A_WIDTH), (1, B_WIDTH), (D_MODEL, D_MODEL), (D_MODEL, D_FF), (D_FF, D_MODEL)]
    spatial_shapes = [(A_HEADS, CHUNK, CHUNK), (CHUNK, A_WIDTH)]

    def const_spec(shape):
        nd = len(shape)
        return pl.BlockSpec(shape, lambda *_: (0,) * nd, pipeline_mode=pl.Buffered(1))

    def layer_spec(shape, layer_of, buffers=1):
        nd = len(shape)
        return pl.BlockSpec((None,) + tuple(shape), lambda *g: (layer_of(*g),) + (0,) * nd,
                            pipeline_mode=pl.Buffered(buffers))

    def weight_specs(layer_of, prefetch=()):
        spec = lambda sh: layer_spec(sh, layer_of, 2 if sh in prefetch else 1)
        return ([spec(sh) for sh in head_shapes] + [const_spec((B_WIDTH, B_WIDTH))]
                + [spec(sh) for sh in spatial_shapes + tail_shapes])

    cparams = functools.partial(pltpu.CompilerParams, vmem_limit_bytes=VMEM_LIMIT_BYTES)

    n_slabs = batch * n_tiles
    assert all(w.shape[1] % (n_slabs * 16) == 0 for w in big_f32)
    slab_shape = lambda w, n=n_slabs: (None, w.shape[1] // n, w.shape[2])
    n_big = len(big_f32)

    def cast_first_layer_kernel(*refs):
        n_in = n_big + len(GAINED)
        _cast_slabs(refs[:n_big], refs[n_big:n_in], refs[n_in:])

    wb = pl.pallas_call(
        cast_first_layer_kernel,
        grid=(FIRST_CAST_STEPS,),
        in_specs=[pl.BlockSpec(slab_shape(w, FIRST_CAST_STEPS), lambda s: (0, s, 0)) for w in big_f32 + big_gain],
        out_specs=[pl.BlockSpec(slab_shape(w, FIRST_CAST_STEPS), lambda s: (0, s, 0)) for w in big_f32],
        out_shape=[jax.ShapeDtypeStruct(w.shape, _bf16) for w in big_f32],
        compiler_params=cparams(dimension_semantics=("arbitrary",)),
        name="cast_first_layer",
    )(*big_f32, *big_gain)

    n_fixed_in = 3 + len(head_shapes) + 1 + len(spatial_shapes) + len(tail_shapes)
    big_in_pos = {0: 3, 1: 12, 2: 13, 3: 14}

    def prompt_layer(l, xp, wb):
        cast_next = l + 1 < depth
        slab_of = lambda b, i: (l + 1, b * n_tiles + i, 0)
        in_specs = ([pl.BlockSpec(memory_space=pltpu.SMEM),
                     pl.BlockSpec((None, PROMPT_TILE, D_MODEL), lambda b, i: (b, i, 0)),
                     pl.BlockSpec((PROMPT_TILE, 3 * LANES), lambda b, i: (i, 0))]
                    + weight_specs(lambda b, i: l))
        assert len(in_specs) == n_fixed_in
        out_specs = [pl.BlockSpec((None, PROMPT_TILE, D_MODEL), lambda b, i: (b, i, 0)),
                     pl.BlockSpec((None, WINDOW, KV_COLS), lambda b, i: (b, 0, 0)),
                     pl.BlockSpec((None, WINDOW, KV_COLS), lambda b, i: (b, 0, 0))]
        out_shape = [jax.ShapeDtypeStruct((batch, seq, D_MODEL), _f32),
                     jax.ShapeDtypeStruct((batch, WINDOW, KV_COLS), _f32),
                     jax.ShapeDtypeStruct((batch, WINDOW, KV_COLS), _f32)]
        args = (sinks[l], xp, rope_p, *head(wb), ones_bd, ws.astype(_bf16), bias, *tail(wb))
        aliases = {}
        if cast_next:
            in_specs += [pl.BlockSpec(slab_shape(w), slab_of) for w in big_f32 + big_gain]
            out_specs += [pl.BlockSpec(slab_shape(w), slab_of) for w in big_f32]
            out_shape += [jax.ShapeDtypeStruct(w.shape, _bf16) for w in big_f32]
            args += big_f32 + big_gain
            aliases = {big_in_pos[j]: 3 + j for j in range(n_big)}
        outs = pl.pallas_call(
            functools.partial(_prompt_kernel, n_cast=n_big if cast_next else 0),
            grid=(batch, n_tiles),
            in_specs=in_specs, out_specs=out_specs, out_shape=out_shape,
            input_output_aliases=aliases,
            scratch_shapes=[pltpu.VMEM((2, CHUNK, KV_COLS), _bf16), pltpu.VMEM((2, CHUNK, KV_COLS), _bf16)],
            compiler_params=cparams(dimension_semantics=("arbitrary", "arbitrary")),
            name="prompt_layer",
        )(*args)
        return outs[0], outs[1], outs[2], (tuple(outs[3:]) if cast_next else wb)

    xp = x_prompt
    nk_p, nv_p = [], []
    for l in range(depth):
        xp, kp, vp, wb = prompt_layer(l, xp, wb)
        nk_p.append(kp)
        nv_p.append(vp)

    win_t = lambda c: jnp.transpose(c, (0, 1, 3, 4, 2)).reshape(depth, dec_batch, KV_COLS, WINDOW)
    win = lambda c: jnp.transpose(c.reshape(depth, dec_batch, KV_HEADS, HEAD_DIM, WINDOW), (0, 1, 4, 2, 3))
    layer_tile = lambda l, i: (l, i, 0, 0)
    ys, ks, vs, va_s = pl.pallas_call(
        _sample_kernel,
        grid=(depth, s_tiles),
        in_specs=[pl.BlockSpec((s_rows, D_MODEL), lambda l, i: (jnp.where(l == 0, i, s_tiles - 1), 0)),
                  const_spec((s_rows, 3 * LANES)),
                  pl.BlockSpec((None, SAMPLE_BATCH_TILE, KV_COLS, WINDOW), layer_tile),
                  pl.BlockSpec((None, SAMPLE_BATCH_TILE, KV_COLS, WINDOW), layer_tile),
                  layer_spec((KV_HEADS, GQA_GROUP * dec_seq, 1), lambda l, i: l)]
                 + weight_specs(lambda l, i: l, prefetch=SAMPLE_PREFETCH_SHAPES),
        out_specs=[pl.BlockSpec((s_rows, D_MODEL), lambda l, i: (jnp.where(l == depth - 1, i, 0), 0)),
                   pl.BlockSpec((None, SAMPLE_BATCH_TILE, KV_COLS, WINDOW), layer_tile),
                   pl.BlockSpec((None, SAMPLE_BATCH_TILE, KV_COLS, WINDOW), layer_tile),
                   pl.BlockSpec((None, s_rows, A_HEADS, HEAD_DIM), lambda l, i: (l, i, 0, 0))],
        out_shape=[jax.ShapeDtypeStruct((dec_batch * dec_seq, D_MODEL), _f32),
                   jax.ShapeDtypeStruct((depth, dec_batch, KV_COLS, WINDOW), _f32),
                   jax.ShapeDtypeStruct((depth, dec_batch, KV_COLS, WINDOW), _f32),
                   jax.ShapeDtypeStruct((depth, dec_batch * dec_seq, A_HEADS, HEAD_DIM), _f32)],
        scratch_shapes=[pltpu.VMEM((s_tiles, s_rows, D_MODEL), _f32)],
        compiler_params=cparams(dimension_semantics=("arbitrary", "arbitrary")),
        name="sample_layers",
    )(x_sample.reshape(dec_batch * dec_seq, D_MODEL), rope_s,
      win_t(cache_win_k), win_t(cache_win_v), sink_rows,
      *head(wb), ones_bd, ws_s.astype(_bf16), bias_s, *tail(wb))

    kv5 = lambda t, n: t.reshape(depth, n, WINDOW, KV_HEADS, HEAD_DIM)
    return (xp, ys.reshape(dec_batch, dec_seq, D_MODEL),
            kv5(jnp.stack(nk_p), batch), kv5(jnp.stack(nv_p), batch), win(ks), win(vs),
            va_s.reshape(depth, dec_batch, dec_seq, A_HEADS, HEAD_DIM))
```

```python
import functools

import jax
import jax.numpy as jnp
import numpy as np
from jax import lax
from jax.experimental import pallas as pl
from jax.experimental.pallas import tpu as pltpu

D_MODEL = 1024
HEAD_DIM = 64
A_WIDTH = 512
B_WIDTH = 512
A_HEADS = A_WIDTH // HEAD_DIM
N_HEADS = B_WIDTH // HEAD_DIM
KV_HEADS = 2
GQA_GROUP = N_HEADS // KV_HEADS
KV_COLS = KV_HEADS * HEAD_DIM
CHUNK = 128
WINDOW = 128
ROPE_THETA = 500000.0
ROT_DIM = HEAD_DIM // 4
D_FF = 4 * D_MODEL
EPS = 1e-6
IN_COLS = 2 * A_WIDTH + B_WIDTH + 2 * KV_COLS
Q_OFF = 2 * A_WIDTH
K_OFF = Q_OFF + B_WIDTH
V_OFF = K_OFF + KV_COLS
NEG_BIG = -1e30
LOG2_E = float(np.log2(np.e))

LANES = 128
FF_CHUNK = 1024
PROMPT_TILE = 512
SAMPLE_BATCH_TILE = 16
PV_LAG = 1
SCORE_LOOKAHEAD = 2
SAMPLE_PREFETCH_SHAPES = ((D_MODEL, IN_COLS), (D_MODEL, D_MODEL), (D_MODEL, D_FF))
FIRST_CAST_STEPS = 8
VMEM_LIMIT_BYTES = 56 * 1024 * 1024

_bf16 = jnp.bfloat16
_f32 = jnp.float32


def _dot(a, b):
    return jnp.dot(a, b, preferred_element_type=_f32)


def _rms(x, g, eps=EPS):
    return x * lax.rsqrt(jnp.mean(x * x, axis=-1, keepdims=True) + eps) * g


def _gelu_x2(x):
    return x * (1.0 + lax.erf(x * np.float32(np.sqrt(0.5))))


def _head_rms(z, g_tiled, ones_bd, eps=EPS):
    ss = _dot((z * z).astype(_bf16), ones_bd)
    return z * lax.rsqrt(ss * (1.0 / HEAD_DIM) + eps) * g_tiled


def _rope(x, cos_t, sin_a, sin_b):
    cols = []
    for c in range(x.shape[1] // LANES):
        xc = x[:, c * LANES:(c + 1) * LANES]
        nxt = pltpu.roll(xc, LANES - ROT_DIM // 2, 1)
        prv = pltpu.roll(xc, ROT_DIM // 2, 1)
        cols.append(xc * cos_t + nxt * sin_a + prv * sin_b)
    return cols[0] if len(cols) == 1 else jnp.concatenate(cols, axis=1)


def _inv_rms(x):
    return lax.rsqrt(jnp.mean(x * x, axis=-1, keepdims=True) + EPS)


def _mixer_inputs_small(x, rope_ref, w_in, g_sv, g_q, g_k, ones_bd):
    z = _dot((x * _inv_rms(x)).astype(_bf16), w_in[...])
    u = _gelu_x2(z[:, :A_WIDTH])
    va = _rms(_gelu_x2(z[:, A_WIDTH:2 * A_WIDTH]), g_sv[...], 4.0 * EPS)
    cos_t, sin_a, sin_b = (rope_ref[:, n * LANES:(n + 1) * LANES] for n in range(3))
    ones = ones_bd[...]
    q = _rope(_head_rms(z[:, Q_OFF:K_OFF], g_q[...], ones), cos_t, sin_a, sin_b)
    k = _rope(_head_rms(z[:, K_OFF:V_OFF], g_k[...], ones[:KV_COLS, :KV_COLS]), cos_t, sin_a, sin_b)
    return u, va, q, k, z[:, V_OFF:]


def _mixer_inputs(x, rope_ref, w_in, g_sv, g_q, g_k, ones_bd):
    xn = x.astype(_bf16)
    mean_sq = jnp.mean(x * x, axis=-1, keepdims=True) + EPS
    r_mix = lax.rsqrt(mean_sq)
    z_qkv = _dot(xn, w_in[:, Q_OFF:])
    cos_t, sin_a, sin_b = (rope_ref[:, n * LANES:(n + 1) * LANES] for n in range(3))
    ones = ones_bd[...]
    head_eps = EPS * mean_sq
    q = _rope(_head_rms(z_qkv[:, :B_WIDTH], g_q[...], ones, head_eps), cos_t, sin_a, sin_b)
    k = _rope(_head_rms(z_qkv[:, B_WIDTH:B_WIDTH + KV_COLS], g_k[...], ones[:KV_COLS, :KV_COLS], head_eps),
              cos_t, sin_a, sin_b)
    v = z_qkv[:, B_WIDTH + KV_COLS:] * r_mix
    va = _rms(_gelu_x2(_dot(xn, w_in[:, A_WIDTH:Q_OFF]) * r_mix), g_sv[...], 4.0 * EPS)
    u = _gelu_x2(_dot(xn, w_in[:, :A_WIDTH]) * r_mix)
    return u, va, q, k, v


def _spatial(va_blk, u_blk, ws_ref, bias):
    vb = va_blk.astype(_bf16)
    outs = [_dot(ws_ref[h], vb[:, h * HEAD_DIM:(h + 1) * HEAD_DIM]) for h in range(A_HEADS)]
    return u_blk * (jnp.concatenate(outs, axis=1) + bias)


def _merge(x, ya, yb, g_oa, g_ob, w_out):
    cat = jnp.concatenate([_rms(ya, g_oa[...]), _rms(yb, g_ob[...])], axis=1).astype(_bf16)
    return x + _dot(cat, w_out[...])


def _ffn(x, w_up, w_down):
    x16 = x.astype(_bf16)
    acc = None
    for c in range(D_FF // FF_CHUNK):
        h = jnp.maximum(_dot(x16, w_up[:, c * FF_CHUNK:(c + 1) * FF_CHUNK]), 0.0)
        part = _dot((h * h).astype(_bf16), w_down[c * FF_CHUNK:(c + 1) * FF_CHUNK, :])
        acc = part if acc is None else acc + part
    r = _inv_rms(x)
    return x + acc * (r * r)


GAIN_WIDTHS = (A_WIDTH, B_WIDTH, KV_COLS, A_WIDTH, B_WIDTH)
GAINED = (0, 2)


def _cast_slabs(srcs, gains, dsts):
    gain_of = dict(zip(GAINED, gains))
    for j, (src, dst) in enumerate(zip(srcs, dsts)):
        val = src[...] * gain_of[j][...] if j in gain_of else src[...]
        dst[...] = val.astype(_bf16)


def _dot_nt(a, b):
    return lax.dot_general(a, b, (((1,), (1,)), ((), ())), preferred_element_type=_f32)


def _swap_halves(t):
    return pltpu.roll(t, HEAD_DIM, 1)


def _lane_lo(shape):
    return lax.broadcasted_iota(jnp.int32, shape, len(shape) - 1) < HEAD_DIM


def _row_gains(gains_ref):
    bounds = np.cumsum((0,) + GAIN_WIDTHS)
    return tuple(gains_ref.at[:, int(a):int(b)] for a, b in zip(bounds[:-1], bounds[1:]))


def _prompt_kernel(sinks_ref, x_ref, rope_ref, w_in, gains_ref, ones_bd, ws_ref,
                   bias_ref, w_out, w_up, w_down, *rest, n_cast):
    g_sv, g_q, g_k, g_oa, g_ob = _row_gains(gains_ref)
    n_gain = len(GAINED) if n_cast else 0
    cast_src, cast_gain, rest = rest[:n_cast], rest[n_cast:n_cast + n_gain], rest[n_cast + n_gain:]
    y_ref, ks_ref, vs_ref = rest[:3]
    cast_dst, (kprev, vprev) = rest[3:3 + n_cast], rest[3 + n_cast:]
    _cast_slabs(cast_src, cast_gain, cast_dst)

    i = pl.program_id(1)
    tile = x_ref.shape[0]
    n_blk = tile // CHUNK

    @pl.when(i == 0)
    def _():
        kprev[...] = jnp.zeros_like(kprev)
        vprev[...] = jnp.zeros_like(vprev)

    x = x_ref[...]
    u, va, q, k, v = _mixer_inputs(x, rope_ref, w_in, g_sv, g_q, g_k, ones_bd)
    even_lanes = lax.broadcasted_iota(jnp.int32, q.shape, 1) % LANES < HEAD_DIM
    q_heads = (jnp.where(even_lanes, q, 0.0).astype(_bf16), jnp.where(even_lanes, 0.0, q).astype(_bf16))
    ks_ref[...] = k[tile - WINDOW:, :]
    vs_ref[...] = v[tile - WINDOW:, :]

    bias = bias_ref[...]
    row = lax.broadcasted_iota(jnp.int32, (CHUNK, 2 * CHUNK), 0)
    col = lax.broadcasted_iota(jnp.int32, (CHUNK, 2 * CHUNK), 1)
    lo = _lane_lo((CHUNK, LANES))

    ones_kv = jnp.ones((2 * CHUNK, LANES), _bf16)
    ya_blocks, yb_blocks = [], []
    for blk in range(n_blk):
        r0, r1 = blk * CHUNK, (blk + 1) * CHUNK
        ya_blocks.append(_spatial(va[r0:r1], u[r0:r1], ws_ref, bias))

        k_blk, v_blk = k[r0:r1], v[r0:r1]
        k_new = (k_blk.astype(_bf16), _swap_halves(k_blk).astype(_bf16))
        v_new = (v_blk.astype(_bf16), _swap_halves(v_blk).astype(_bf16))
        if blk == 0:
            k_old = (kprev[0], kprev[1])
            v_old = (vprev[0], vprev[1])
            first = jnp.where(i == 0, 2 * CHUNK, 0)
        else:
            first = 0
        valid = jnp.where(col < CHUNK, col - row - first, row + CHUNK - col) >= 0
        k_ctx = [jnp.concatenate([k_old[a], k_new[a]], axis=0) for a in range(2)]
        v_ctx = [jnp.concatenate([jnp.concatenate([v_old[a], v_new[a]], axis=0), ones_kv], axis=1)
                 for a in range(2)]
        k_old, v_old = k_new, v_new

        def arrangement(h):
            return (h // GQA_GROUP + h % 2) % 2

        def scores(h):
            qm = q_heads[h % 2][r0:r1, (h // 2) * LANES:(h // 2 + 1) * LANES]
            return _dot_nt(qm, k_ctx[arrangement(h)])

        outs = []
        ahead = [scores(h) for h in range(SCORE_LOOKAHEAD)]
        behind = []

        def finish(hh, e, sink_term):
            o = _dot(e, v_ctx[arrangement(hh)])
            outs.append(o[:, :LANES] * (1.0 / (o[:, LANES:] + sink_term)))

        for h in range(N_HEADS):
            sc = jnp.where(valid, ahead.pop(0), NEG_BIG)
            if h + SCORE_LOOKAHEAD < N_HEADS:
                ahead.append(scores(h + SCORE_LOOKAHEAD))
            sink = sinks_ref[h] * LOG2_E
            m = jnp.maximum(jnp.max(sc, axis=-1, keepdims=True), sink)
            behind.append((h, jnp.exp2(sc - m).astype(_bf16), jnp.exp2(sink - m)))
            if len(behind) > PV_LAG:
                finish(*behind.pop(0))
        while behind:
            finish(*behind.pop(0))
        yb_blocks.append(jnp.concatenate(
            [jnp.where(lo, outs[2 * p], outs[2 * p + 1]) for p in range(N_HEADS // 2)], axis=1))

    kprev[0], kprev[1] = k_old
    vprev[0], vprev[1] = v_old
    ya = jnp.concatenate(ya_blocks, axis=0)
    yb = jnp.concatenate(yb_blocks, axis=0)
    y_ref[...] = _ffn(_merge(x, ya, yb, g_oa, g_ob, w_out), w_up, w_down)


def _sample_kernel(x_ref, rope_ref, ck_ref, cv_ref, sink_rows, w_in, gains_ref, ones_bd,
                   ws_ref, bias_ref, w_out, w_up, w_down,
                   y_ref, ko_ref, vo_ref, va_ref, x_all):
    g_sv, g_q, g_k, g_oa, g_ob = _row_gains(gains_ref)
    l = pl.program_id(0)
    i = pl.program_id(1)
    rows = x_ref.shape[0]
    nb = ck_ref.shape[0]
    dec_seq = rows // nb
    grp_rows = GQA_GROUP * dec_seq

    @pl.when(l == 0)
    def _():
        x_all[i] = x_ref[...]

    x = x_all[i]
    u, va, q, k, v = _mixer_inputs_small(x, rope_ref, w_in, g_sv, g_q, g_k, ones_bd)
    for h in range(A_HEADS):
        va_ref[:, h, :] = va[:, h * HEAD_DIM:(h + 1) * HEAD_DIM]

    bias = bias_ref[...]
    ya = jnp.concatenate(
        [_spatial(va[r:r + CHUNK], u[r:r + CHUNK], ws_ref, bias) for r in range(0, rows, CHUNK)], axis=0)

    ck, cv = ck_ref[...], cv_ref[...]
    k_t, v_t = k.T, v.T
    lane = lax.broadcasted_iota(jnp.int32, (KV_COLS, WINDOW), 1)
    for b in range(nb):
        col = b * dec_seq
        grp, off = col // LANES, col % LANES
        for new_t, cache, out_ref in ((k_t, ck, ko_ref), (v_t, cv, vo_ref)):
            fresh = pltpu.roll(new_t[:, grp * LANES:(grp + 1) * LANES], (LANES - off) % LANES, 1)
            merged = jnp.where(lane < dec_seq, fresh, cache[b])
            out_ref[b] = pltpu.roll(merged, WINDOW - dec_seq, 1)

    ck16, cv16 = ck.astype(_bf16), cv.astype(_bf16)
    kt16 = k_t.astype(_bf16)
    q3 = q.reshape(nb, dec_seq, B_WIDTH)
    t_c = lax.broadcasted_iota(jnp.int32, (grp_rows, WINDOW), 0) % dec_seq
    w_c = lax.broadcasted_iota(jnp.int32, (grp_rows, WINDOW), 1)
    valid_c = (w_c >= t_c)[None]
    b_n = lax.broadcasted_iota(jnp.int32, (nb, grp_rows, rows), 0)
    t_n = lax.broadcasted_iota(jnp.int32, (nb, grp_rows, rows), 1) % dec_seq
    c_n = lax.broadcasted_iota(jnp.int32, (nb, grp_rows, rows), 2)
    t_key = c_n - b_n * dec_seq
    valid_n = jnp.minimum(t_key, t_n - t_key) >= 0

    heads = []
    for kv in range(KV_HEADS):
        d0, d1 = kv * HEAD_DIM, (kv + 1) * HEAD_DIM
        q_st = jnp.concatenate(
            [q3[:, :, h * HEAD_DIM:(h + 1) * HEAD_DIM] for h in range(kv * GQA_GROUP, (kv + 1) * GQA_GROUP)],
            axis=1).astype(_bf16)
        s_c = jnp.einsum('bqd,bdw->bqw', q_st, ck16[:, d0:d1, :], preferred_element_type=_f32)
        s_n = _dot(q_st.reshape(nb * grp_rows, HEAD_DIM), kt16[d0:d1, :]).reshape(nb, grp_rows, rows)
        s_c = jnp.where(valid_c, s_c, NEG_BIG)
        s_n = jnp.where(valid_n, s_n, NEG_BIG)
        sink = sink_rows[kv][None]
        m = jnp.maximum(jnp.maximum(jnp.max(s_c, axis=-1, keepdims=True),
                                    jnp.max(s_n, axis=-1, keepdims=True)), sink)
        e_c, e_n = jnp.exp2(s_c - m), jnp.exp2(s_n - m)
        denom = (jnp.sum(e_c, axis=-1, keepdims=True) + jnp.sum(e_n, axis=-1, keepdims=True)
                 + jnp.exp2(sink - m))
        o = jnp.einsum('bqw,bdw->bqd', e_c.astype(_bf16), cv16[:, d0:d1, :], preferred_element_type=_f32)
        o = o + _dot(e_n.reshape(nb * grp_rows, rows).astype(_bf16),
                     v[:, d0:d1].astype(_bf16)).reshape(nb, grp_rows, HEAD_DIM)
        o = o * (1.0 / denom)
        heads += [o[:, g * dec_seq:(g + 1) * dec_seq, :] for g in range(GQA_GROUP)]
    yb = jnp.concatenate(heads, axis=2).reshape(rows, B_WIDTH)
    y = _ffn(_merge(x, ya, yb, g_oa, g_ob, w_out), w_up, w_down)
    x_all[i] = y
    y_ref[...] = y


def _rope_tables(pos):
    half = ROT_DIM // 2
    inv = np.power(np.float32(ROPE_THETA), -2.0 * np.arange(half, dtype=np.float32) / np.float32(ROT_DIM))
    ang = pos.astype(np.float32)[:, None] * inv.astype(np.float32)[None, :]
    cos, sin = np.cos(ang).astype(np.float32), np.sin(ang).astype(np.float32)
    n = pos.shape[0]
    pad1 = np.ones((n, HEAD_DIM - ROT_DIM), np.float32)
    pad0 = np.zeros((n, HEAD_DIM - ROT_DIM), np.float32)
    zero = np.zeros((n, half), np.float32)
    cos_t = np.concatenate([cos, cos, pad1], axis=1)
    sin_a = np.concatenate([-sin, zero, pad0], axis=1)
    sin_b = np.concatenate([zero, sin, pad0], axis=1)
    rep = LANES // HEAD_DIM
    return np.concatenate([np.tile(t, (1, rep)) for t in (cos_t, sin_a, sin_b)], axis=1)


def kernel(x_prompt, x_sample, cache_win_k, cache_win_v, g_mix, w_in, g_sv, g_q, g_k, w_spatial,
           b_spatial, sinks, g_out_a, g_out_b, w_out, g_ffn, w_up, w_down):
    batch, seq, _ = x_prompt.shape
    dec_batch, dec_seq, _ = x_sample.shape
    depth = w_in.shape[0]
    assert seq % PROMPT_TILE == 0 and PROMPT_TILE % CHUNK == 0
    assert dec_batch % SAMPLE_BATCH_TILE == 0 and CHUNK % dec_seq == 0
    s_rows = SAMPLE_BATCH_TILE * dec_seq
    assert s_rows % CHUNK == 0
    n_tiles = seq // PROMPT_TILE
    s_tiles = dec_batch // SAMPLE_BATCH_TILE

    rope_p = jnp.asarray(_rope_tables(np.arange(seq)))
    rope_s = jnp.asarray(np.tile(_rope_tables(seq + np.arange(dec_seq)), (SAMPLE_BATCH_TILE, 1)))
    hid = np.arange(B_WIDTH) // HEAD_DIM
    ones_bd = jnp.asarray(hid[:, None] == hid[None, :], dtype=_bf16)
    tril = np.tril(np.ones((CHUNK, CHUNK), dtype=bool))
    rep = CHUNK // dec_seq
    expand = jnp.asarray(np.tile(np.eye(dec_seq, dtype=np.float32), (rep, 1)))
    same_batch = jnp.asarray(np.kron(np.eye(rep), np.ones((dec_seq, dec_seq))), dtype=_f32)

    row3 = lambda a: a[:, None, :]
    ws = jnp.where(tril, 0.5 * w_spatial, 0.0)
    bias = jnp.repeat(jnp.swapaxes(0.5 * b_spatial, 1, 2), HEAD_DIM, axis=2)
    ws_s = jnp.einsum('rt,lhts,cs->lhrc', expand, ws[:, :, :dec_seq, :dec_seq], expand,
                      precision=lax.Precision.HIGHEST) * same_batch
    bias_s = jnp.tile(bias[:, :dec_seq], (1, rep, 1))
    sink_rows = jnp.repeat((sinks * LOG2_E).reshape(depth, KV_HEADS, GQA_GROUP), dec_seq, axis=2)[..., None]
    big_f32 = (w_in, w_out, w_up, w_down)
    col3 = lambda a: a[:, :, None]
    big_gain = (col3(g_mix), col3(g_ffn))
    g_q_scaled = jnp.tile(g_q, (1, N_HEADS)) * (HEAD_DIM ** -0.5 * LOG2_E)
    gains_row = row3(jnp.concatenate([g_sv, g_q_scaled, jnp.tile(g_k, (1, KV_HEADS)), g_out_a, g_out_b], axis=1))
    head = lambda wb: (wb[0], gains_row)
    tail = lambda wb: (wb[1], wb[2], wb[3])
    head_shapes = [(D_MODEL, IN_COLS), (1, sum(GAIN_WIDTHS))]
    tail_shapes = [(D_MODEL, D_MODEL), (D_MODEL, D_FF), (D_FF, D_MODEL)]
    spatial_shapes = [(A_HEADS, CHUNK, CHUNK), (CHUNK, A_WIDTH)]

    def const_spec(shape):
        nd = len(shape)
        return pl.BlockSpec(shape, lambda *_: (0,) * nd, pipeline_mode=pl.Buffered(1))

    def layer_spec(shape, layer_of, buffers=1):
        nd = len(shape)
        return pl.BlockSpec((None,) + tuple(shape), lambda *g: (layer_of(*g),) + (0,) * nd,
                            pipeline_mode=pl.Buffered(buffers))

    def weight_specs(layer_of, prefetch=()):
        spec = lambda sh: layer_spec(sh, layer_of, 2 if sh in prefetch else 1)
        return ([spec(sh) for sh in head_shapes] + [const_spec((B_WIDTH, B_WIDTH))]
                + [spec(sh) for sh in spatial_shapes + tail_shapes])

    cparams = functools.partial(pltpu.CompilerParams, vmem_limit_bytes=VMEM_LIMIT_BYTES)

    n_slabs = batch * n_tiles
    assert all(w.shape[1] % (n_slabs * 16) == 0 for w in big_f32)
    slab_shape = lambda w, n=n_slabs: (None, w.shape[1] // n, w.shape[2])
    n_big = len(big_f32)

    def cast_first_layer_kernel(*refs):
        n_in = n_big + len(GAINED)
        _cast_slabs(refs[:n_big], refs[n_big:n_in], refs[n_in:])

    wb = pl.pallas_call(
        cast_first_layer_kernel,
        grid=(FIRST_CAST_STEPS,),
        in_specs=[pl.BlockSpec(slab_shape(w, FIRST_CAST_STEPS), lambda s: (0, s, 0)) for w in big_f32 + big_gain],
        out_specs=[pl.BlockSpec(slab_shape(w, FIRST_CAST_STEPS), lambda s: (0, s, 0)) for w in big_f32],
        out_shape=[jax.ShapeDtypeStruct(w.shape, _bf16) for w in big_f32],
        compiler_params=cparams(dimension_semantics=("arbitrary",)),
        name="cast_first_layer",
    )(*big_f32, *big_gain)

    n_fixed_in = 3 + len(head_shapes) + 1 + len(spatial_shapes) + len(tail_shapes)
    big_in_pos = {0: 3, 1: 8, 2: 9, 3: 10}

    def prompt_layer(l, xp, wb):
        cast_next = l + 1 < depth
        slab_of = lambda b, i: (l + 1, b * n_tiles + i, 0)
        in_specs = ([pl.BlockSpec(memory_space=pltpu.SMEM),
                     pl.BlockSpec((None, PROMPT_TILE, D_MODEL), lambda b, i: (b, i, 0)),
                     pl.BlockSpec((PROMPT_TILE, 3 * LANES), lambda b, i: (i, 0))]
                    + weight_specs(lambda b, i: l))
        assert len(in_specs) == n_fixed_in
        out_specs = [pl.BlockSpec((None, PROMPT_TILE, D_MODEL), lambda b, i: (b, i, 0)),
                     pl.BlockSpec((None, WINDOW, KV_COLS), lambda b, i: (b, 0, 0)),
                     pl.BlockSpec((None, WINDOW, KV_COLS), lambda b, i: (b, 0, 0))]
        out_shape = [jax.ShapeDtypeStruct((batch, seq, D_MODEL), _f32),
                     jax.ShapeDtypeStruct((batch, WINDOW, KV_COLS), _f32),
                     jax.ShapeDtypeStruct((batch, WINDOW, KV_COLS), _f32)]
        args = (sinks[l], xp, rope_p, *head(wb), ones_bd, ws.astype(_bf16), bias, *tail(wb))
        aliases = {}
        if cast_next:
            in_specs += [pl.BlockSpec(slab_shape(w), slab_of) for w in big_f32 + big_gain]
            out_specs += [pl.BlockSpec(slab_shape(w), slab_of) for w in big_f32]
            out_shape += [jax.ShapeDtypeStruct(w.shape, _bf16) for w in big_f32]
            args += big_f32 + big_gain
            aliases = {big_in_pos[j]: 3 + j for j in range(n_big)}
        outs = pl.pallas_call(
            functools.partial(_prompt_kernel, n_cast=n_big if cast_next else 0),
            grid=(batch, n_tiles),
            in_specs=in_specs, out_specs=out_specs, out_shape=out_shape,
            input_output_aliases=aliases,
            scratch_shapes=[pltpu.VMEM((2, CHUNK, KV_COLS), _bf16), pltpu.VMEM((2, CHUNK, KV_COLS), _bf16)],
            compiler_params=cparams(dimension_semantics=("arbitrary", "arbitrary")),
            name="prompt_layer",
        )(*args)
        return outs[0], outs[1], outs[2], (tuple(outs[3:]) if cast_next else wb)

    xp = x_prompt
    nk_p, nv_p = [], []
    for l in range(depth):
        xp, kp, vp, wb = prompt_layer(l, xp, wb)
        nk_p.append(kp)
        nv_p.append(vp)

    win_t = lambda c: jnp.transpose(c, (0, 1, 3, 4, 2)).reshape(depth, dec_batch, KV_COLS, WINDOW)
    win = lambda c: jnp.transpose(c.reshape(depth, dec_batch, KV_HEADS, HEAD_DIM, WINDOW), (0, 1, 4, 2, 3))
    layer_tile = lambda l, i: (l, i, 0, 0)
    ys, ks, vs, va_s = pl.pallas_call(
        _sample_kernel,
        grid=(depth, s_tiles),
        in_specs=[pl.BlockSpec((s_rows, D_MODEL), lambda l, i: (jnp.where(l == 0, i, s_tiles - 1), 0)),
                  const_spec((s_rows, 3 * LANES)),
                  pl.BlockSpec((None, SAMPLE_BATCH_TILE, KV_COLS, WINDOW), layer_tile),
                  pl.BlockSpec((None, SAMPLE_BATCH_TILE, KV_COLS, WINDOW), layer_tile),
                  layer_spec((KV_HEADS, GQA_GROUP * dec_seq, 1), lambda l, i: l)]
                 + weight_specs(lambda l, i: l, prefetch=SAMPLE_PREFETCH_SHAPES),
        out_specs=[pl.BlockSpec((s_rows, D_MODEL), lambda l, i: (jnp.where(l == depth - 1, i, 0), 0)),
                   pl.BlockSpec((None, SAMPLE_BATCH_TILE, KV_COLS, WINDOW), layer_tile),
                   pl.BlockSpec((None, SAMPLE_BATCH_TILE, KV_COLS, WINDOW), layer_tile),
                   pl.BlockSpec((None, s_rows, A_HEADS, HEAD_DIM), lambda l, i: (l, i, 0, 0))],
        out_shape=[jax.ShapeDtypeStruct((dec_batch * dec_seq, D_MODEL), _f32),
                   jax.ShapeDtypeStruct((depth, dec_batch, KV_COLS, WINDOW), _f32),
                   jax.ShapeDtypeStruct((depth, dec_batch, KV_COLS, WINDOW), _f32),
                   jax.ShapeDtypeStruct((depth, dec_batch * dec_seq, A_HEADS, HEAD_DIM), _f32)],
        scratch_shapes=[pltpu.VMEM((s_tiles, s_rows, D_MODEL), _f32)],
        compiler_params=cparams(dimension_semantics=("arbitrary", "arbitrary")),
        name="sample_layers",
    )(x_sample.reshape(dec_batch * dec_seq, D_MODEL), rope_s,
      win_t(cache_win_k), win_t(cache_win_v), sink_rows,
      *head(wb), ones_bd, ws_s.astype(_bf16), bias_s, *tail(wb))

    kv5 = lambda t, n: t.reshape(depth, n, WINDOW, KV_HEADS, HEAD_DIM)
    return (xp, ys.reshape(dec_batch, dec_seq, D_MODEL),
            kv5(jnp.stack(nk_p), batch), kv5(jnp.stack(nv_p), batch), win(ks), win(vs),
            va_s.reshape(depth, dec_batch, dec_seq, A_HEADS, HEAD_DIM))
```
